```python
import jax
import jax.numpy as jnp
from jax import lax
import numpy as np

D_MODEL = 2048
BATCH = 2
SEQ = 8192
DEPTH = 4

GRID_W = 64
CTX_LEN = 256
N_EVEN = (DEPTH + 1) // 2
N_ODD = DEPTH // 2
N_MOD = 6

DN_HEADS = 8
DN_DK = 128
DN_DV = 128
DN_CHUNK = 64
SHORT_CONV = 4
DN_QK = DN_HEADS * DN_DK
DN_VW = DN_HEADS * DN_DV

LRU_WIDTH = 1024
LRU_BLOCKS = 8
LRU_BW = LRU_WIDTH // LRU_BLOCKS
LRU_CONV = 4
LRU_C = 8.0

MLA_HEADS = 16
MLA_KV_RANK = 512
MLA_NOPE = 128
MLA_ROPE = 64
MLA_V = 128
MLA_QD = MLA_NOPE + MLA_ROPE
ROPE_FREQS = MLA_ROPE // 4
ROPE_BASE = 10000.0
Q_BLOCK = 128

N_GROUPS = 4
EXP_PER_GROUP = 8
N_EXPERTS = N_GROUPS * EXP_PER_GROUP
TOP_K = 2
D_EXPERT = 512
MOE_BLOCK = 128

DEEPNORM_ALPHA = (2.0 * DEPTH) ** 0.25
DEEPNORM_BETA = (8.0 * DEPTH) ** -0.25
LN_EPS = 1e-5
NORM_EPS = 1e-6

EVEN_SIZES = (2 * DN_QK + DN_VW, DN_VW, 2 * DN_HEADS, 2 * DN_HEADS, LRU_WIDTH, LRU_WIDTH)
EVEN_IN = sum(EVEN_SIZES)
EVEN_SPLITS = tuple(int(s) for s in np.cumsum(EVEN_SIZES)[:-1])
EVEN_MIX = DN_VW + LRU_WIDTH
ODD_SIZES = (MLA_HEADS * MLA_QD, MLA_KV_RANK, MLA_ROPE)
ODD_IN = sum(ODD_SIZES)
ODD_SPLITS = tuple(int(s) for s in np.cumsum(ODD_SIZES)[:-1])
ODD_MIX = MLA_HEADS * MLA_V

kernel_name = 'hybrid_prefix_dit_block'


def layer_norm(x, g, b):
    xf = x.astype(jnp.float32)
    mu = jnp.mean(xf, -1, keepdims=True)
    xc = xf - mu
    var = jnp.mean(xc * xc, -1, keepdims=True)
    return (xc * lax.rsqrt(var + LN_EPS) * g.astype(jnp.float32) + b.astype(jnp.float32)).astype(x.dtype)


def post_norm(x, y, g, b):
    return layer_norm(DEEPNORM_ALPHA * x + y, g, b)


def rms_norm(x, g):
    xf = x.astype(jnp.float32)
    return xf * lax.rsqrt(jnp.mean(xf * xf, -1, keepdims=True) + NORM_EPS) * g.astype(jnp.float32)


def l2_normalize(x):
    xf = x.astype(jnp.float32)
    return xf * lax.rsqrt(jnp.sum(xf * xf, -1, keepdims=True) + NORM_EPS)


def modulate(x, shift, scale):
    return x * (1.0 + scale) + shift


def centred_dwconv(x, w):
    k, ch = w.shape
    return lax.conv_general_dilated(x, w[:, None, :].astype(x.dtype), window_strides=(1,),
                                    padding=[(k // 2, (k - 1) // 2)],
                                    dimension_numbers=('NWC', 'WIO', 'NWC'), feature_group_count=ch)


def axial_rope_tables(n):
    rows = n // GRID_W
    row = jnp.broadcast_to(jnp.arange(rows)[:, None], (rows, GRID_W)).reshape(-1)
    col = jnp.broadcast_to(jnp.arange(GRID_W)[None, :], (rows, GRID_W)).reshape(-1)
    pos = jnp.stack([row, col], -1).astype(jnp.float32)
    inv = ROPE_BASE ** (-jnp.arange(ROPE_FREQS, dtype=jnp.float32) / ROPE_FREQS)
    ang = pos[:, :, None] * inv
    return jnp.cos(ang)[:, :, None, :], jnp.sin(ang)[:, :, None, :]


def apply_axial_rope(x, cos, sin):
    shp = x.shape
    xr = x.reshape(shp[:-1] + (2, 2, ROPE_FREQS))
    rot = jnp.stack([-xr[..., 1, :], xr[..., 0, :]], axis=-2)
    return (xr * cos.astype(x.dtype) + rot * sin.astype(x.dtype)).reshape(shp)


def gated_delta_chunked(q, k, v, g, beta, s0):
    bsz, seq, nh, dk = q.shape
    dv = v.shape[-1]
    n = seq // DN_CHUNK
    f32 = jnp.float32

    def blocks(t):
        t = t.astype(f32).reshape((bsz, n, DN_CHUNK, nh) + t.shape[3:])
        return jnp.moveaxis(t, 3, 1)

    qb, kb, vb, gb, bb = blocks(q), blocks(k), blocks(v), blocks(g), blocks(beta)
    gcum = jnp.cumsum(gb, axis=-1)
    incl = jnp.tril(jnp.ones((DN_CHUNK, DN_CHUNK), bool))
    strict = jnp.tril(jnp.ones((DN_CHUNK, DN_CHUNK), bool), -1)
    diff = gcum[..., :, None] - gcum[..., None, :]
    decay = jnp.where(incl, jnp.exp(jnp.where(incl, diff, 0.0)), 0.0)
    k_beta = kb * bb[..., None]
    a_kk = jnp.where(strict, jnp.einsum('bhnid,bhnjd->bhnij', k_beta, kb) * decay, 0.0)
    rhs = jnp.concatenate([vb * bb[..., None], k_beta * jnp.exp(gcum)[..., None]], -1)
    sol = lax.linalg.triangular_solve(a_kk + jnp.eye(DN_CHUNK, dtype=f32), rhs, left_side=True, lower=True)
    u, w = sol[..., :dv], sol[..., dv:]
    a_qk = jnp.where(incl, jnp.einsum('bhnid,bhnjd->bhnij', qb, kb) * decay, 0.0)
    q_dec = qb * jnp.exp(gcum)[..., None]
    k_dec = kb * jnp.exp(gcum[..., -1:] - gcum)[..., None]
    g_tot = jnp.exp(gcum[..., -1])

    def step(s, xs):
        q_i, k_i, u_i, w_i, a_i, gt_i = xs
        v_new = u_i - jnp.einsum('bhck,bhkv->bhcv', w_i, s)
        o_i = jnp.einsum('bhck,bhkv->bhcv', q_i, s) + jnp.einsum('bhij,bhjv->bhiv', a_i, v_new)
        s = s * gt_i[..., None, None] + jnp.einsum('bhck,bhcv->bhkv', k_i, v_new)
        return s, o_i

    xs = tuple(jnp.moveaxis(t, 2, 0) for t in (q_dec, k_dec, u, w, a_qk, g_tot))
    s_fin, o = lax.scan(step, s0.astype(f32), xs)
    o = jnp.moveaxis(jnp.moveaxis(o, 0, 2), 1, 3).reshape(bsz, seq, nh, dv)
    return o, s_fin


def linear_combine(e1, e2):
    a1, b1 = e1
    a2, b2 = e2
    return a1 * a2, a2 * b1 + b2


def block_diag_linear(x, w, b):
    bsz, seq, width = x.shape
    xb = x.reshape(bsz, seq, LRU_BLOCKS, LRU_BW)
    return jnp.einsum('blnd,nde->blne', xb, w).reshape(bsz, seq, width) + b


def prefix_bidirectional(step_fn, ctx_args, lat_args, init):
    outs_c, outs_l = [], []
    for d in range(2):
        flip = (lambda t: jnp.flip(t, axis=1)) if d == 1 else (lambda t: t)
        oc, s = step_fn(d, [flip(t) for t in ctx_args], init)
        ol, _ = step_fn(d, [flip(t) for t in lat_args], s)
        outs_c.append(flip(oc))
        outs_l.append(flip(ol))
    return outs_c[0] + outs_c[1], outs_l[0] + outs_l[1]


def even_mixer(hc, hl, w_in, conv_qkv, a_log, dt_bias, o_norm, conv_x_w, conv_x_b,
               w_r, b_r, w_i, b_i, lam, w_out, need_ctx):
    f32 = jnp.float32

    def prep(h):
        bsz, seq, _ = h.shape
        qkv, z, a, b, xr, yr = jnp.split(h @ w_in, EVEN_SPLITS, axis=-1)
        qkv = jax.nn.silu(centred_dwconv(qkv, conv_qkv))
        q, k, v = jnp.split(qkv, [DN_QK, 2 * DN_QK], axis=-1)
        q = l2_normalize(q.reshape(bsz, seq, DN_HEADS, DN_DK)) * (DN_DK ** -0.5)
        k = l2_normalize(k.reshape(bsz, seq, DN_HEADS, DN_DK))
        v = v.reshape(bsz, seq, DN_HEADS, DN_DV)
        g = -jnp.exp(a_log.astype(f32)) * jax.nn.softplus(
            a.reshape(bsz, seq, 2, DN_HEADS).astype(f32) + dt_bias.astype(f32))
        beta = jax.nn.sigmoid(b.reshape(bsz, seq, 2, DN_HEADS).astype(f32))
        u = centred_dwconv(xr, conv_x_w) + conv_x_b
        return [q, k, v, g, beta], [u], z, yr

    dn_c, lru_c, zc, yc = prep(hc)
    dn_l, lru_l, zl, yl = prep(hl)
    bsz = hl.shape[0]

    def dn_dir(d, args, s0):
        q, k, v, g, beta = args
        return gated_delta_chunked(q, k, v, g[:, :, d], beta[:, :, d], s0)

    def lru_dir(d, args, h0):
        uf = args[0].astype(f32)
        r = jax.nn.sigmoid(block_diag_linear(uf, w_r[d], b_r[d]))
        gi = jax.nn.sigmoid(block_diag_linear(uf, w_i[d], b_i[d]))
        log_a = -LRU_C * jax.nn.softplus(-lam[d].astype(f32)) * r
        a = jnp.exp(log_a)
        bterm = jnp.sqrt(-jnp.expm1(2.0 * log_a)) * (gi * uf)
        bterm = bterm.at[:, 0].add(a[:, 0] * h0)
        _, h = lax.associative_scan(linear_combine, (a, bterm), axis=1)
        return h, h[:, -1]

    dno_c, dno_l = prefix_bidirectional(dn_dir, dn_c, dn_l, jnp.zeros((bsz, DN_HEADS, DN_DK, DN_DV), f32))
    lro_c, lro_l = prefix_bidirectional(lru_dir, lru_c, lru_l, jnp.zeros((bsz, LRU_WIDTH), f32))

    def output(dn_o, z, lru_h, y):
        b_, seq = z.shape[:2]
        o = rms_norm(dn_o, o_norm) * jax.nn.silu(z.reshape(b_, seq, DN_HEADS, DN_DV).astype(f32))
        mix = jnp.concatenate([o.reshape(b_, seq, DN_VW), lru_h * jax.nn.gelu(y.astype(f32))], -1)
        return mix.astype(z.dtype) @ w_out

    ol = output(dno_l, zl, lro_l, yl)
    oc = output(dno_c, zc, lro_c, yc) if need_ctx else None
    return oc, ol


def block_softmax_attention(q, k, v):
    bsz, nq, nh, dq = q.shape
    nb = nq // Q_BLOCK
    scale = dq ** -0.5
    qb = jnp.moveaxis(q.reshape(bsz, nb, Q_BLOCK, nh, dq), 1, 0)

    def one(qblk):
        s = jnp.einsum('bqhd,bkhd->bhqk', qblk, k, preferred_element_type=jnp.float32) * scale
        p = jax.nn.softmax(s, axis=-1).astype(v.dtype)
        return jnp.einsum('bhqk,bkhd->bqhd', p, v)

    o = lax.map(one, qb)
    return jnp.moveaxis(o, 0, 1).reshape(bsz, nq, nh, v.shape[-1])


def mla_mixer(hc, hl, w_in, kv_norm, w_ukv, w_out, rope, need_ctx):
    def project(h, rot):
        bsz, seq, _ = h.shape
        q, ckv, kr = jnp.split(h @ w_in, ODD_SPLITS, axis=-1)
        q = q.reshape(bsz, seq, MLA_HEADS, MLA_QD)
        kv = (rms_norm(ckv, kv_norm).astype(h.dtype) @ w_ukv).reshape(bsz, seq, MLA_HEADS, MLA_NOPE + MLA_V)
        k_nope, v = kv[..., :MLA_NOPE], kv[..., MLA_NOPE:]
        q_nope, q_rope = q[..., :MLA_NOPE], q[..., MLA_NOPE:]
        if rot is not None:
            cos, sin = rot
            q_rope = apply_axial_rope(q_rope, cos[:, None], sin[:, None])
            kr = apply_axial_rope(kr, cos, sin)
        q = jnp.concatenate([q_nope, q_rope], -1)
        k = jnp.concatenate([k_nope, jnp.broadcast_to(kr[:, :, None, :], (bsz, seq, MLA_HEADS, MLA_ROPE))], -1)
        return q, k, v

    qc, kc, vc = project(hc, None)
    ql, kl, vl = project(hl, rope)
    k_all = jnp.concatenate([kc, kl], axis=1)
    v_all = jnp.concatenate([vc, vl], axis=1)
    bsz, n = hl.shape[:2]
    ol = block_softmax_attention(ql, k_all, v_all).reshape(bsz, n, ODD_MIX) @ w_out
    oc = None
    if need_ctx:
        oc = block_softmax_attention(qc, kc, vc).reshape(bsz, hc.shape[1], ODD_MIX) @ w_out
    return oc, ol


def hier_moe(h, w_grp, b_grp, w_exp, b_exp, w1, w3, w2):
    n_tok, dm = h.shape
    lg = (h @ w_grp + b_grp).astype(jnp.float32)
    grp = jnp.argmax(lg, axis=-1)
    p_grp = jnp.take_along_axis(jax.nn.softmax(lg, -1), grp[:, None], -1)
    le = (h @ w_exp + b_exp).astype(jnp.float32).reshape(n_tok, N_GROUPS, EXP_PER_GROUP)
    le = jnp.take_along_axis(le, grp[:, None, None], axis=1)[:, 0]
    top_v, top_i = lax.top_k(le, TOP_K)
    gate = (p_grp * jax.nn.softmax(top_v, -1)).reshape(-1)
    expert = (grp[:, None] * EXP_PER_GROUP + top_i).reshape(-1)
    token = jnp.repeat(jnp.arange(n_tok), TOP_K)
    order = jnp.argsort(expert)
    e_s, t_s, g_s = expert[order], token[order], gate[order]
    counts = jax.ops.segment_sum(jnp.ones_like(expert), expert, num_segments=N_EXPERTS)
    padded = (counts + MOE_BLOCK - 1) // MOE_BLOCK * MOE_BLOCK
    start = jnp.cumsum(counts) - counts
    pend = jnp.cumsum(padded)
    pstart = pend - padded
    dest = pstart[e_s] + jnp.arange(n_tok * TOP_K) - start[e_s]
    n_blk = (n_tok * TOP_K + MOE_BLOCK - 1) // MOE_BLOCK + N_EXPERTS
    buf = jnp.zeros((n_blk * MOE_BLOCK, dm), h.dtype).at[dest].set(h[t_s])
    blk_e = jnp.minimum(jnp.sum(pend[None, :] <= (jnp.arange(n_blk) * MOE_BLOCK)[:, None], axis=1), N_EXPERTS - 1)

    def expert_block(args):
        xb, e = args
        return (jax.nn.silu(xb @ w1[e]) * (xb @ w3[e])) @ w2[e]

    y = lax.map(expert_block, (buf.reshape(n_blk, MOE_BLOCK, dm), blk_e)).reshape(-1, dm)
    return jnp.zeros((n_tok, dm), h.dtype).at[t_s].add(y[dest] * g_s[:, None].astype(h.dtype))


def setup_inputs(seed: int = 0) -> dict:
    key = jax.random.key(seed)
    keys = jax.random.split(key, 40)
    kit = iter(range(40))
    f32 = jnp.float32
    D = D_MODEL

    def nk():
        return keys[next(kit)]

    def nrm(shape, scale):
        return jax.random.normal(nk(), shape, f32) * scale

    def gain(shape):
        return 1.0 + nrm(shape, 0.02)

    x = nrm((BATCH, SEQ, D), 1.0)
    c = nrm((BATCH, D), 1.0)
    ctx = nrm((BATCH, CTX_LEN, D), 1.0)
    c_ctx = nrm((D,), 1.0)
    ada_w = nrm((DEPTH, D, N_MOD * D), D ** -0.5)
    ada_b = nrm((DEPTH, N_MOD * D), 0.02)
    ln_mix_g = gain((DEPTH, D))
    ln_mix_b = nrm((DEPTH, D), 0.02)
    ln_ffn_g = gain((DEPTH, D))
    ln_ffn_b = nrm((DEPTH, D), 0.02)
    ev_w_in = nrm((N_EVEN, D, EVEN_IN), D ** -0.5)
    ev_conv_qkv = nrm((N_EVEN, SHORT_CONV, 2 * DN_QK + DN_VW), SHORT_CONV ** -0.5)
    ev_a_log = jnp.log(jax.random.uniform(nk(), (N_EVEN, 2, DN_HEADS), f32, 1.0, 16.0))
    dt = jnp.exp(jax.random.uniform(nk(), (N_EVEN, 2, DN_HEADS), f32, float(np.log(1e-3)), float(np.log(1e-1))))
    ev_dt_bias = dt + jnp.log(-jnp.expm1(-dt))
    ev_o_norm = gain((N_EVEN, DN_DV))
    ev_conv_x_w = nrm((N_EVEN, LRU_CONV, LRU_WIDTH), LRU_CONV ** -0.5)
    ev_conv_x_b = nrm((N_EVEN, LRU_WIDTH), 0.02)
    ev_w_r = nrm((N_EVEN, 2, LRU_BLOCKS, LRU_BW, LRU_BW), LRU_BW ** -0.5)
    ev_b_r = nrm((N_EVEN, 2, LRU_WIDTH), 0.02)
    ev_w_i = nrm((N_EVEN, 2, LRU_BLOCKS, LRU_BW, LRU_BW), LRU_BW ** -0.5)
    ev_b_i = nrm((N_EVEN, 2, LRU_WIDTH), 0.02)
    a0 = jax.random.uniform(nk(), (N_EVEN, 2, LRU_WIDTH), f32, 0.9, 0.999) ** (1.0 / LRU_C)
    ev_lam = jnp.log(a0) - jnp.log1p(-a0)
    ev_w_out = nrm((N_EVEN, EVEN_MIX, D), EVEN_MIX ** -0.5 * DEEPNORM_BETA)
    od_w_in = nrm((N_ODD, D, ODD_IN), D ** -0.5)
    od_kv_norm = gain((N_ODD, MLA_KV_RANK))
    od_w_ukv = nrm((N_ODD, MLA_KV_RANK, MLA_HEADS * (MLA_NOPE + MLA_V)), MLA_KV_RANK ** -0.5)
    od_w_out = nrm((N_ODD, ODD_MIX, D), ODD_MIX ** -0.5 * DEEPNORM_BETA)
    moe_w_grp = nrm((DEPTH, D, N_GROUPS), D ** -0.5)
    moe_b_grp = nrm((DEPTH, N_GROUPS), 0.01)
    moe_w_exp = nrm((DEPTH, D, N_EXPERTS), D ** -0.5)
    moe_b_exp = nrm((DEPTH, N_EXPERTS), 0.01)
    moe_w1 = nrm((DEPTH, N_EXPERTS, D, D_EXPERT), D ** -0.5)
    moe_w3 = nrm((DEPTH, N_EXPERTS, D, D_EXPERT), D ** -0.5)
    moe_w2 = nrm((DEPTH, N_EXPERTS, D_EXPERT, D), D_EXPERT ** -0.5 * DEEPNORM_BETA)
    return {'x': x, 'c': c, 'ctx': ctx, 'c_ctx': c_ctx, 'ada_w': ada_w, 'ada_b': ada_b,
            'ln_mix_g': ln_mix_g, 'ln_mix_b': ln_mix_b, 'ln_ffn_g': ln_ffn_g, 'ln_ffn_b': ln_ffn_b,
            'ev_w_in': ev_w_in, 'ev_conv_qkv': ev_conv_qkv, 'ev_a_log': ev_a_log, 'ev_dt_bias': ev_dt_bias,
            'ev_o_norm': ev_o_norm, 'ev_conv_x_w': ev_conv_x_w, 'ev_conv_x_b': ev_conv_x_b,
            'ev_w_r': ev_w_r, 'ev_b_r': ev_b_r, 'ev_w_i': ev_w_i, 'ev_b_i': ev_b_i, 'ev_lam': ev_lam,
            'ev_w_out': ev_w_out, 'od_w_in': od_w_in, 'od_kv_norm': od_kv_norm, 'od_w_ukv': od_w_ukv,
            'od_w_out': od_w_out, 'moe_w_grp': moe_w_grp, 'moe_b_grp': moe_b_grp, 'moe_w_exp': moe_w_exp,
            'moe_b_exp': moe_b_exp, 'moe_w1': moe_w1, 'moe_w3': moe_w3, 'moe_w2': moe_w2}


def reference(x, c, ctx, c_ctx, ada_w, ada_b, ln_mix_g, ln_mix_b, ln_ffn_g, ln_ffn_b,
              ev_w_in, ev_conv_qkv, ev_a_log, ev_dt_bias, ev_o_norm, ev_conv_x_w, ev_conv_x_b,
              ev_w_r, ev_b_r, ev_w_i, ev_b_i, ev_lam, ev_w_out,
              od_w_in, od_kv_norm, od_w_ukv, od_w_out,
              moe_w_grp, moe_b_grp, moe_w_exp, moe_b_exp, moe_w1, moe_w3, moe_w2):
    bsz, n, dm = x.shape
    n_ctx = ctx.shape[1]
    xl, xc = x, ctx
    rope = axial_rope_tables(n)
    for layer in range(DEPTH):
        need_ctx = layer < DEPTH - 1
        i = layer // 2
        mod_l = jnp.split((jax.nn.silu(c) @ ada_w[layer] + ada_b[layer])[:, None, :], N_MOD, axis=-1)
        mod_c = jnp.split(jax.nn.silu(c_ctx) @ ada_w[layer] + ada_b[layer], N_MOD, axis=-1)
        hl = modulate(xl, mod_l[0], mod_l[1])
        hc = modulate(xc, mod_c[0], mod_c[1])
        if layer % 2 == 0:
            oc, ol = even_mixer(hc, hl, ev_w_in[i], ev_conv_qkv[i], ev_a_log[i], ev_dt_bias[i], ev_o_norm[i],
                                ev_conv_x_w[i], ev_conv_x_b[i], ev_w_r[i], ev_b_r[i], ev_w_i[i], ev_b_i[i],
                                ev_lam[i], ev_w_out[i], need_ctx)
        else:
            oc, ol = mla_mixer(hc, hl, od_w_in[i], od_kv_norm[i], od_w_ukv[i], od_w_out[i], rope, need_ctx)
        xl = post_norm(xl, mod_l[2] * ol, ln_mix_g[layer], ln_mix_b[layer])
        hl = modulate(xl, mod_l[3], mod_l[4])
        moe_args = (moe_w_grp[layer], moe_b_grp[layer], moe_w_exp[layer], moe_b_exp[layer],
                    moe_w1[layer], moe_w3[layer], moe_w2[layer])
        if need_ctx:
            xc = post_norm(xc, mod_c[2] * oc, ln_mix_g[layer], ln_mix_b[layer])
            hc = modulate(xc, mod_c[3], mod_c[4])
            f = hier_moe(jnp.concatenate([hc.reshape(-1, dm), hl.reshape(-1, dm)], axis=0), *moe_args)
            fc = f[:bsz * n_ctx].reshape(xc.shape)
            fl = f[bsz * n_ctx:].reshape(xl.shape)
            xc = post_norm(xc, mod_c[5] * fc, ln_ffn_g[layer], ln_ffn_b[layer])
        else:
            fl = hier_moe(hl.reshape(-1, dm), *moe_args).reshape(xl.shape)
        xl = post_norm(xl, mod_l[5] * fl, ln_ffn_g[layer], ln_ffn_b[layer])
    return xl
```

```python
import functools
import math

import jax
import jax.numpy as jnp
from jax import lax
from jax.experimental import pallas as pl
from jax.experimental.pallas import tpu as pltpu

F32 = jnp.float32
BF16 = jnp.bfloat16
HIGHEST = lax.Precision.HIGHEST

LANES = 128
DN_DK = 128
DN_DV = 128
DN_CHUNK = 64
MLA_NOPE = 128
MLA_ROPE = 64
MLA_V = 128
MLA_QD = MLA_NOPE + MLA_ROPE
GRID_W = 64
ROPE_FREQS = MLA_ROPE // 4
ROPE_BASE = 10000.0
LRU_C = 8.0
LN_EPS = 1e-5
NORM_EPS = 1e-6
TOP_K = 2
VMEM_LIMIT = 48 * 1024 * 1024


def _pick(n, cands):
    for c in cands:
        if n % c == 0:
            return c
    raise ValueError(f"no tile for {n} in {cands}")


def _params(*sem):
    return pltpu.CompilerParams(dimension_semantics=sem, vmem_limit_bytes=VMEM_LIMIT)


def _dot(a, b, precision=None):
    return jnp.dot(a, b, preferred_element_type=F32, precision=precision)


def _dot_nt(a, b, precision=None):
    return lax.dot_general(a, b, (((1,), (1,)), ((), ())), preferred_element_type=F32, precision=precision)


def _sigmoid(x):
    return 1.0 / (1.0 + jnp.exp(-x))


def _silu(x):
    return x * _sigmoid(x)


def _softplus(x):
    return jnp.maximum(x, 0.0) + jnp.log1p(jnp.exp(-jnp.abs(x)))


def _expm1_nonpos(x):
    u = jnp.exp(x)
    safe = (x > -0.5) & (u < 1.0)
    stable = (u - 1.0) * x / jnp.log(jnp.where(safe, u, 0.5))
    return jnp.where(safe, stable, jnp.where(u < 1.0, u - 1.0, x))


def _adaln_kernel(c_ref, w_ref, b_ref, o_ref):
    a = _silu(c_ref[...])
    o_ref[0] = _dot(a.astype(BF16), w_ref[0].astype(BF16)) + b_ref[0]


def _adaln(c_all, ada_w, ada_b):
    n_layer, d, n6 = ada_w.shape
    tn = _pick(n6, (1024, 512, 256, 128))
    return pl.pallas_call(
        _adaln_kernel, grid=(n_layer, n6 // tn),
        in_specs=[pl.BlockSpec((8, d), lambda l, j: (0, 0)),
                  pl.BlockSpec((1, d, tn), lambda l, j: (l, 0, j)),
                  pl.BlockSpec((1, 1, tn), lambda l, j: (l, 0, j))],
        out_specs=pl.BlockSpec((1, 8, tn), lambda l, j: (l, 0, j)),
        out_shape=jax.ShapeDtypeStruct((n_layer, 8, n6), F32),
        compiler_params=_params("arbitrary", "arbitrary"), name="adaln",
    )(c_all, ada_w, ada_b.reshape(n_layer, 1, n6))


def _seg_spec(d, n_lat_blocks):
    return pl.BlockSpec((1, 1, 1, d), lambda b, i: (b, jnp.where(i >= n_lat_blocks, 1, 0), 0, 0))


def _modulate_kernel(x_ref, sh_ref, sc_ref, h_ref):
    h_ref[0] = (x_ref[0] * (1.0 + sc_ref[0, 0]) + sh_ref[0, 0]).astype(BF16)


def _modulate(x, shift, scale, r, n_lat):
    b, s, d = x.shape
    row = pl.BlockSpec((1, r, d), lambda b_, i: (b_, i, 0))
    return pl.pallas_call(
        _modulate_kernel, grid=(b, s // r),
        in_specs=[row, _seg_spec(d, n_lat // r), _seg_spec(d, n_lat // r)],
        out_specs=row, out_shape=jax.ShapeDtypeStruct((b, s, d), BF16),
        compiler_params=_params("arbitrary", "arbitrary"), name="modulate",
    )(x, shift, scale)


def _postnorm_kernel(*refs, alpha, with_h, with_router):
    x_ref, y_ref, gate_ref, g_ref, b_ref = refs[:5]
    pos = 5
    if with_h:
        sh_ref, sc_ref = refs[pos:pos + 2]
        pos += 2
    if with_router:
        wr_ref, br_ref = refs[pos:pos + 2]
        pos += 2
    xo_ref = refs[pos]
    v = alpha * x_ref[0] + gate_ref[0, 0] * y_ref[0].astype(F32)
    mu = jnp.mean(v, axis=-1, keepdims=True)
    vc = v - mu
    var = jnp.mean(vc * vc, axis=-1, keepdims=True)
    xn = vc * lax.rsqrt(var + LN_EPS) * g_ref[...] + b_ref[...]
    xo_ref[0] = xn
    if with_h:
        h = xn * (1.0 + sc_ref[0, 0]) + sh_ref[0, 0]
        refs[pos + 1][0] = h.astype(BF16)
        if with_router:
            refs[pos + 2][0] = _dot(h, wr_ref[...], HIGHEST) + br_ref[...]


def _postnorm(x, y, gate, ln_g, ln_b, r, n_lat, alpha, shift=None, scale=None, router=None, rows=None):
    b, s, d = x.shape
    rows = s if rows is None else rows
    nlb = n_lat // r
    row = pl.BlockSpec((1, r, d), lambda b_, i: (b_, i, 0))
    vec = pl.BlockSpec((1, d), lambda b_, i: (0, 0))
    args = [x, y, gate, ln_g.reshape(1, d), ln_b.reshape(1, d)]
    in_specs = [row, row, _seg_spec(d, nlb), vec, vec]
    out_shape = [jax.ShapeDtypeStruct((b, rows, d), F32)]
    out_specs = [row]
    with_h = shift is not None
    if with_h:
        args += [shift, scale]
        in_specs += [_seg_spec(d, nlb), _seg_spec(d, nlb)]
        out_shape.append(jax.ShapeDtypeStruct((b, rows, d), BF16))
        out_specs.append(row)
    if router is not None:
        w_r, b_r = router
        nr = w_r.shape[1]
        args += [w_r, b_r]
        in_specs += [pl.BlockSpec((d, nr), lambda b_, i: (0, 0)), pl.BlockSpec((1, nr), lambda b_, i: (0, 0))]
        out_shape.append(jax.ShapeDtypeStruct((b, rows, nr), F32))
        out_specs.append(pl.BlockSpec((1, r, nr), lambda b_, i: (b_, i, 0)))
    return pl.pallas_call(
        functools.partial(_postnorm_kernel, alpha=alpha, with_h=with_h, with_router=router is not None),
        grid=(b, rows // r), in_specs=in_specs, out_specs=out_specs, out_shape=out_shape,
        compiler_params=_params("arbitrary", "arbitrary"), name="postnorm",
    )(*args)


def _mm_kernel(a_ref, b_ref, o_ref):
    o_ref[...] = _dot(a_ref[...].astype(BF16), b_ref[...].astype(BF16)).astype(o_ref.dtype)


def _mm(a, b, out_dtype=F32):
    m, k = a.shape
    n = b.shape[1]
    tm = _pick(m, (512, 256, 128, 64, 8))
    tn = _pick(n, (1024, 896, 768, 640, 512, 384, 256, 128))
    return pl.pallas_call(
        _mm_kernel, grid=(m // tm, n // tn),
        in_specs=[pl.BlockSpec((tm, k), lambda i, j: (i, 0)), pl.BlockSpec((k, tn), lambda i, j: (0, j))],
        out_specs=pl.BlockSpec((tm, tn), lambda i, j: (i, j)),
        out_shape=jax.ShapeDtypeStruct((m, n), out_dtype),
        compiler_params=_params("arbitrary", "arbitrary"), name="mm",
    )(a, b)


def _even_prep_kernel(qkv_ref, qkv_p_ref, qkv_n_ref, xr_ref, xr_p_ref, xr_n_ref, ab_ref,
                      cw_ref, xw_ref, xb_ref, nea_ref, dtb_ref,
                      q_ref, k_ref, v_ref, u_ref, gb_ref, *, n_lat_blocks, n_blocks, heads, r):
    i = pl.program_id(1)
    pv = jnp.where((i != 0) & (i != n_lat_blocks), 1.0, 0.0)
    nv = jnp.where((i != n_lat_blocks - 1) & (i != n_blocks - 1), 1.0, 0.0)
    row = lax.broadcasted_iota(jnp.int32, (r, 1), 0)

    def conv(x, p8, n8, w):
        p8 = p8 * pv
        n8 = n8 * nv
        xm1 = jnp.where(row == 0, p8[7:8], pltpu.roll(x, 1, 0))
        xm2 = jnp.where(row == 0, p8[6:7], jnp.where(row == 1, p8[7:8], pltpu.roll(x, 2, 0)))
        xp1 = jnp.where(row == r - 1, n8[0:1], pltpu.roll(x, r - 1, 0))
        return w[0:1] * xm2 + w[1:2] * xm1 + w[2:3] * x + w[3:4] * xp1

    for j in range(3 * heads):
        sl = slice(j * LANES, (j + 1) * LANES)
        y = _silu(conv(qkv_ref[0, :, sl], qkv_p_ref[0, :, sl], qkv_n_ref[0, :, sl], cw_ref[:, sl]))
        if j < 2 * heads:
            y = y * lax.rsqrt(jnp.sum(y * y, axis=-1, keepdims=True) + NORM_EPS)
        if j < heads:
            q_ref[0, :, sl] = y * (DN_DK ** -0.5)
        elif j < 2 * heads:
            k_ref[0, :, slice((j - heads) * LANES, (j - heads + 1) * LANES)] = y
        else:
            v_ref[0, :, slice((j - 2 * heads) * LANES, (j - 2 * heads + 1) * LANES)] = y
    for j in range(xr_ref.shape[2] // LANES):
        sl = slice(j * LANES, (j + 1) * LANES)
        u_ref[0, :, sl] = conv(xr_ref[0, :, sl], xr_p_ref[0, :, sl], xr_n_ref[0, :, sl], xw_ref[:, sl]) + xb_ref[:, sl]
    ab = ab_ref[0]
    lane = lax.broadcasted_iota(jnp.int32, ab.shape, 1)
    g = nea_ref[...] * _softplus(ab + dtb_ref[...])
    gb_ref[0] = jnp.where(lane < 2 * heads, g, _sigmoid(ab))


def _even_prep(p, conv_qkv, conv_x_w, conv_x_b, neg_exp_a, dt_bias, r, n_lat, heads, width):
    b, s, _ = p.shape
    vw = heads * DN_DV
    qkvw = 3 * vw
    assert width == vw and qkvw % width == 0
    nb, nlb, r8 = s // r, n_lat // r, r // 8
    n8 = s // 8

    def cur(wd, cb):
        return pl.BlockSpec((1, r, wd), lambda b_, i: (b_, i, cb))

    def prev(wd, cb):
        return pl.BlockSpec((1, 8, wd), lambda b_, i: (b_, jnp.maximum(i * r8 - 1, 0), cb))

    def nxt(wd, cb):
        return pl.BlockSpec((1, 8, wd), lambda b_, i: (b_, jnp.minimum((i + 1) * r8, n8 - 1), cb))

    def par(shape):
        return pl.BlockSpec(shape, lambda b_, i: (0, 0))

    xcb = (qkvw + vw) // width
    abcb = (qkvw + vw + 2 * width) // LANES
    pad = LANES - 2 * heads
    nea = jnp.pad(neg_exp_a.reshape(1, 2 * heads), ((0, 0), (0, pad)))
    dtb = jnp.pad(dt_bias.reshape(1, 2 * heads), ((0, 0), (0, pad)))
    outs = [jax.ShapeDtypeStruct((b, s, vw), F32)] * 3 + [jax.ShapeDtypeStruct((b, s, width), F32),
                                                            jax.ShapeDtypeStruct((b, s, LANES), F32)]
    return pl.pallas_call(
        functools.partial(_even_prep_kernel, n_lat_blocks=nlb, n_blocks=nb, heads=heads, r=r),
        grid=(b, nb),
        in_specs=[cur(qkvw, 0), prev(qkvw, 0), nxt(qkvw, 0), cur(width, xcb), prev(width, xcb), nxt(width, xcb),
                  cur(LANES, abcb), par((4, qkvw)), par((4, width)), par((1, width)), par((1, LANES)), par((1, LANES))],
        out_specs=[cur(vw, 0), cur(vw, 0), cur(vw, 0), cur(width, 0), cur(LANES, 0)],
        out_shape=outs, compiler_params=_params("arbitrary", "arbitrary"), name="even_prep",
    )(p, p, p, p, p, p, p, conv_qkv, conv_x_w, conv_x_b.reshape(1, width), nea, dtb)


def _delta_kernel(qf, kf, vf, gf, qb, kb, vb, gbk, of, ob, state, *, hb):
    c = DN_CHUNK

    @pl.when(pl.program_id(2) == 0)
    def _():
        state[...] = jnp.zeros_like(state)

    ri = lax.broadcasted_iota(jnp.int32, (c, c), 0)
    ci = lax.broadcasted_iota(jnp.int32, (c, c), 1)
    eye = jnp.where(ri == ci, 1.0, 0.0)
    for d, (q_ref, k_ref, v_ref, g_ref, o_ref) in enumerate(((qf, kf, vf, gf, of), (qb, kb, vb, gbk, ob))):
        incl = (ri >= ci) if d == 0 else (ri <= ci)
        strict = (ri > ci) if d == 0 else (ri < ci)
        last = c - 1 if d == 0 else 0
        levels = [strict & ((ri >> (k + 1)) == (ci >> (k + 1))) & ((ri >> k) != (ci >> k)) for k in range(6)]
        gall = g_ref[0, 0]
        gcum = _dot(jnp.where(incl, 1.0, 0.0), gall, HIGHEST)
        gcum_t = gcum.T
        for hh in range(hb):
            col = d * hb + hh
            sl = slice(hh * LANES, (hh + 1) * LANES)
            gc = gcum[:, col:col + 1]
            gr = gcum_t[col:col + 1, :]
            g_last = gcum[last:last + 1, col:col + 1]
            beta = gall[:, 2 * hb + col:2 * hb + col + 1]
            decay = jnp.where(incl, jnp.exp(jnp.where(incl, gc - gr, 0.0)), 0.0)
            qh, kh, vh = q_ref[0, :, sl], k_ref[0, :, sl], v_ref[0, :, sl]
            k_beta = kh * beta
            a_kk = jnp.where(strict, _dot_nt(k_beta, kh) * decay, 0.0)
            t = eye - jnp.where(levels[0], a_kk, 0.0)
            for k in range(1, 6):
                t = t - _dot(_dot(t, jnp.where(levels[k], a_kk, 0.0), HIGHEST), t, HIGHEST)
            eg = jnp.exp(gc)
            sol = _dot(t, jnp.concatenate([vh * beta, k_beta * eg], axis=1), HIGHEST)
            u, w = sol[:, :DN_DV], sol[:, DN_DV:]
            a_qk = jnp.where(incl, _dot_nt(qh, kh) * decay, 0.0)
            s_prev = state[d, hh]
            v_new = u - _dot(w, s_prev)
            o_ref[0, :, sl] = _dot(qh * eg, s_prev) + _dot(a_qk, v_new)
            k_dec = kh * jnp.exp(g_last - gc)
            state[d, hh] = s_prev * jnp.exp(g_last) + _dot(k_dec.T, v_new)


def _delta(q, k, v, gates, n_lat, heads, hb):
    b, s, vw = q.shape
    c = DN_CHUNK
    n_chunks, n_lat_c = s // c, n_lat // c
    n_ctx_c = n_chunks - n_lat_c
    groups = heads // hb

    def fwd(st):
        return jnp.where(st < n_ctx_c, n_lat_c + st, st - n_ctx_c)

    def bwd(st):
        return jnp.where(st < n_ctx_c, n_chunks - 1 - st, n_chunks - 1 - st)

    def bwd_idx(st):
        return jnp.where(st < n_ctx_c, n_chunks - 1 - st, n_lat_c - 1 - (st - n_ctx_c))

    def seq(order):
        return pl.BlockSpec((1, c, hb * LANES), lambda b_, g, st: (b_, order(st), g))

    def gat(order):
        return pl.BlockSpec((1, 1, c, LANES), lambda b_, g, st: (b_, g, order(st), 0))

    del bwd
    out = jax.ShapeDtypeStruct((b, s, vw), F32)
    return pl.pallas_call(
        functools.partial(_delta_kernel, hb=hb), grid=(b, groups, n_chunks),
        in_specs=[seq(fwd), seq(fwd), seq(fwd), gat(fwd), seq(bwd_idx), seq(bwd_idx), seq(bwd_idx), gat(bwd_idx)],
        out_specs=[seq(fwd), seq(bwd_idx)], out_shape=[out, out],
        scratch_shapes=[pltpu.VMEM((2, hb, DN_DK, DN_DV), F32)],
        compiler_params=_params("arbitrary", "arbitrary", "arbitrary"), name="delta",
    )(q, k, v, gates, q, k, v, gates)


def _lru_kernel(uf_ref, ub_ref, wri_ref, bri_ref, spl_ref, hf_ref, hb_ref, carry, *, r, n_blk):
    @pl.when(pl.program_id(1) == 0)
    def _():
        carry[...] = jnp.zeros_like(carry)

    row = lax.broadcasted_iota(jnp.int32, (r, 1), 0)
    for d, (u_ref, h_ref) in enumerate(((uf_ref, hf_ref), (ub_ref, hb_ref))):
        for n in range(n_blk):
            sl = slice(n * LANES, (n + 1) * LANES)
            u = u_ref[0, :, sl]
            ri = _dot(u.astype(BF16), wri_ref[d, n]) + bri_ref[d, n]
            rg = _sigmoid(ri[:, :LANES])
            ig = _sigmoid(ri[:, LANES:])
            log_a = -spl_ref[d, n] * rg
            a = jnp.exp(log_a)
            bt = jnp.sqrt(-_expm1_nonpos(2.0 * log_a)) * (ig * u)
            sh = 1
            while sh < r:
                if d == 0:
                    keep = row >= sh
                    a_s, b_s = pltpu.roll(a, sh, 0), pltpu.roll(bt, sh, 0)
                else:
                    keep = row < r - sh
                    a_s, b_s = pltpu.roll(a, r - sh, 0), pltpu.roll(bt, r - sh, 0)
                bt = a * jnp.where(keep, b_s, 0.0) + bt
                a = a * jnp.where(keep, a_s, 1.0)
                sh *= 2
            h = bt + a * carry[d, :, sl]
            h_ref[0, :, sl] = h
            carry[d, :, sl] = h[r - 1:r] if d == 0 else h[0:1]


def _lru(u, w_ri, b_ri, spl, r, n_lat):
    b, s, width = u.shape
    n_blk = width // LANES
    nb, nlb = s // r, n_lat // r
    ncb = nb - nlb

    def fwd(st):
        return jnp.where(st < ncb, nlb + st, st - ncb)

    def bwd(st):
        return jnp.where(st < ncb, nb - 1 - st, nlb - 1 - (st - ncb))

    def seq(order):
        return pl.BlockSpec((1, r, width), lambda b_, st: (b_, order(st), 0))

    def par(shape):
        return pl.BlockSpec(shape, lambda b_, st: (0,) * len(shape))

    out = jax.ShapeDtypeStruct((b, s, width), F32)
    return pl.pallas_call(
        functools.partial(_lru_kernel, r=r, n_blk=n_blk), grid=(b, nb),
        in_specs=[seq(fwd), seq(bwd), par(w_ri.shape), par(b_ri.shape), par(spl.shape)],
        out_specs=[seq(fwd), seq(bwd)], out_shape=[out, out],
        scratch_shapes=[pltpu.VMEM((2, 1, width), F32)],
        compiler_params=_params("arbitrary", "arbitrary"), name="lru",
    )(u, u, w_ri, b_ri, spl)


def _gelu_tanh(x):
    return 0.5 * x * (1.0 + jnp.tanh(math.sqrt(2.0 / math.pi) * (x + 0.044715 * (x * x * x))))


def _even_out_kernel(of_ref, ob_ref, z_ref, hf_ref, hb_ref, y_ref, on_ref, mix_ref, *, heads, n_blk):
    for h in range(heads):
        sl = slice(h * LANES, (h + 1) * LANES)
        o = of_ref[0, :, sl] + ob_ref[0, :, sl]
        o = o * lax.rsqrt(jnp.mean(o * o, axis=-1, keepdims=True) + NORM_EPS) * on_ref[...]
        mix_ref[0, :, sl] = (o * _silu(z_ref[0, :, sl])).astype(BF16)
    for n in range(n_blk):
        sl = slice(n * LANES, (n + 1) * LANES)
        osl = slice((heads + n) * LANES, (heads + n + 1) * LANES)
        mix_ref[0, :, osl] = ((hf_ref[0, :, sl] + hb_ref[0, :, sl]) * _gelu_tanh(y_ref[0, :, sl])).astype(BF16)


def _even_out(o_f, o_b, p, h_f, h_b, o_norm, r, heads):
    b, s, vw = o_f.shape
    width = h_f.shape[2]

    def cur(wd, cb):
        return pl.BlockSpec((1, r, wd), lambda b_, i: (b_, i, cb))

    zcb = (3 * vw) // vw
    ycb = (4 * vw + width) // width
    return pl.pallas_call(
        functools.partial(_even_out_kernel, heads=heads, n_blk=width // LANES), grid=(b, s // r),
        in_specs=[cur(vw, 0), cur(vw, 0), cur(vw, zcb), cur(width, 0), cur(width, 0), cur(width, ycb),
                  pl.BlockSpec((1, DN_DV), lambda b_, i: (0, 0))],
        out_specs=cur(vw + width, 0), out_shape=jax.ShapeDtypeStruct((b, s, vw + width), BF16),
        compiler_params=_params("arbitrary", "arbitrary"), name="even_out",
    )(o_f, o_b, p, h_f, h_b, p, o_norm.reshape(1, DN_DV))


def _rope_pair(y, tab):
    y = y * tab
    return y + pltpu.roll(y, MLA_ROPE, 1)


def _q_proj_kernel(a_ref, w_ref, tab_ref, q_ref, *, scale):
    acc = _dot(a_ref[0], w_ref[0])
    q_ref[0, 0, :, 0:MLA_NOPE] = (acc[:, :MLA_NOPE] * scale).astype(BF16)
    qr = _rope_pair(acc[:, MLA_NOPE:], tab_ref[...])
    q_ref[0, 0, :, MLA_NOPE:MLA_QD] = (qr[:, :MLA_ROPE] * scale).astype(BF16)


def _q_proj(h, w_q, tab, heads):
    b, s, d = h.shape
    tm = _pick(s, (768, 512, 640, 256, 128))
    return pl.pallas_call(
        functools.partial(_q_proj_kernel, scale=MLA_QD ** -0.5), grid=(b, s // tm, heads),
        in_specs=[pl.BlockSpec((1, tm, d), lambda b_, i, hd: (b_, i, 0)),
                  pl.BlockSpec((1, d, 2 * LANES), lambda b_, i, hd: (hd, 0, 0)),
                  pl.BlockSpec((tm, LANES), lambda b_, i, hd: (i, 0))],
        out_specs=pl.BlockSpec((1, 1, tm, MLA_QD), lambda b_, i, hd: (b_, hd, i, 0)),
        out_shape=jax.ShapeDtypeStruct((b, heads, s, MLA_QD), BF16),
        compiler_params=_params("arbitrary", "arbitrary", "arbitrary"), name="q_proj",
    )(h, w_q, tab)


def _ckv_proj_kernel(a_ref, w_ref, tab_ref, g_ref, ckv_ref, kr_ref, *, rank):
    acc = _dot(a_ref[0], w_ref[...])
    ckv = acc[:, :rank]
    ckv = ckv * lax.rsqrt(jnp.mean(ckv * ckv, axis=-1, keepdims=True) + NORM_EPS) * g_ref[...]
    ckv_ref[0] = ckv.astype(BF16)
    kr_ref[0] = _rope_pair(acc[:, rank:], tab_ref[...]).astype(BF16)


def _ckv_proj(h, w_c, tab, kv_norm):
    b, s, d = h.shape
    rank = kv_norm.shape[0]
    tm = _pick(s, (768, 512, 640, 256, 128))
    return pl.pallas_call(
        functools.partial(_ckv_proj_kernel, rank=rank), grid=(b, s // tm),
        in_specs=[pl.BlockSpec((1, tm, d), lambda b_, i: (b_, i, 0)),
                  pl.BlockSpec((d, rank + LANES), lambda b_, i: (0, 0)),
                  pl.BlockSpec((tm, LANES), lambda b_, i: (i, 0)),
                  pl.BlockSpec((1, rank), lambda b_, i: (0, 0))],
        out_specs=[pl.BlockSpec((1, tm, rank), lambda b_, i: (b_, i, 0)),
                   pl.BlockSpec((1, tm, LANES), lambda b_, i: (b_, i, 0))],
        out_shape=[jax.ShapeDtypeStruct((b, s, rank), BF16), jax.ShapeDtypeStruct((b, s, LANES), BF16)],
        compiler_params=_params("arbitrary", "arbitrary"), name="ckv_proj",
    )(h, w_c, tab, kv_norm.reshape(1, rank))


def _kv_up_kernel(a_ref, w_ref, kr_ref, k_ref, v_ref):
    acc = _dot(a_ref[0], w_ref[0])
    k_ref[0, 0, :, 0:MLA_NOPE] = acc[:, :MLA_NOPE].astype(BF16)
    k_ref[0, 0, :, MLA_NOPE:MLA_QD] = kr_ref[0, :, 0:MLA_ROPE]
    v_ref[0, 0] = acc[:, MLA_NOPE:].astype(BF16)


def _kv_up(ckv, w_ukv, kr, heads):
    b, s, rank = ckv.shape
    tm = _pick(s, (768, 512, 640, 256, 128))
    return pl.pallas_call(
        _kv_up_kernel, grid=(b, s // tm, heads),
        in_specs=[pl.BlockSpec((1, tm, rank), lambda b_, i, hd: (b_, i, 0)),
                  pl.BlockSpec((1, rank, MLA_NOPE + MLA_V), lambda b_, i, hd: (hd, 0, 0)),
                  pl.BlockSpec((1, tm, LANES), lambda b_, i, hd: (b_, i, 0))],
        out_specs=[pl.BlockSpec((1, 1, tm, MLA_QD), lambda b_, i, hd: (b_, hd, i, 0)),
                   pl.BlockSpec((1, 1, tm, MLA_V), lambda b_, i, hd: (b_, hd, i, 0))],
        out_shape=[jax.ShapeDtypeStruct((b, heads, s, MLA_QD), BF16),
                   jax.ShapeDtypeStruct((b, heads, s, MLA_V), BF16)],
        compiler_params=_params("arbitrary", "arbitrary", "arbitrary"), name="kv_up",
    )(ckv, w_ukv, kr)


def _flash_kernel(q_ref, k_ref, v_ref, o_ref, m_ref, l_ref, acc_ref):
    j = pl.program_id(3)

    @pl.when(j == 0)
    def _():
        m_ref[...] = jnp.full_like(m_ref, -jnp.inf)
        l_ref[...] = jnp.zeros_like(l_ref)
        acc_ref[...] = jnp.zeros_like(acc_ref)

    s = _dot_nt(q_ref[0, 0], k_ref[0, 0])
    m_prev = m_ref[...]
    m_new = jnp.maximum(m_prev, jnp.max(s, axis=-1, keepdims=True))
    p = jnp.exp(s - m_new)
    alpha = jnp.exp(m_prev - m_new)
    l_ref[...] = alpha * l_ref[...] + jnp.sum(p, axis=-1, keepdims=True)
    acc_ref[...] = alpha * acc_ref[...] + _dot(p.astype(BF16), v_ref[0, 0])
    m_ref[...] = m_new

    @pl.when(j == pl.num_programs(3) - 1)
    def _():
        o_ref[0] = (acc_ref[...] / l_ref[...]).astype(o_ref.dtype)


def _flash(q, k, v, q_rows, kv_rows):
    b, heads, _, dq = q.shape
    dv = v.shape[3]
    q0, nq = q_rows
    k0, nk = kv_rows
    tq = _pick(math.gcd(nq, q0) if q0 else nq, (1024, 512, 256, 128))
    tk = _pick(math.gcd(nk, k0) if k0 else nk, (768, 512, 256, 128))
    qo, ko = q0 // tq, k0 // tk
    return pl.pallas_call(
        _flash_kernel, grid=(b, heads, nq // tq, nk // tk),
        in_specs=[pl.BlockSpec((1, 1, tq, dq), lambda b_, h, i, j: (b_, h, qo + i, 0)),
                  pl.BlockSpec((1, 1, tk, dq), lambda b_, h, i, j: (b_, h, ko + j, 0)),
                  pl.BlockSpec((1, 1, tk, dv), lambda b_, h, i, j: (b_, h, ko + j, 0))],
        out_specs=pl.BlockSpec((1, tq, dv), lambda b_, h, i, j: (b_, i, h)),
        out_shape=jax.ShapeDtypeStruct((b, nq, heads * dv), BF16),
        scratch_shapes=[pltpu.VMEM((tq, 1), F32), pltpu.VMEM((tq, 1), F32), pltpu.VMEM((tq, dv), F32)],
        compiler_params=_params("arbitrary", "arbitrary", "arbitrary", "arbitrary"), name="flash",
    )(q, k, v)


def _experts_kernel(be_ref, nu_ref, x_ref, w1_ref, w3_ref, w2_ref, g_ref, y_ref):
    @pl.when(pl.program_id(0) < nu_ref[0])
    def _():
        x = x_ref[...]
        h1 = _dot(x, w1_ref[0].astype(BF16))
        h3 = _dot(x, w3_ref[0].astype(BF16))
        hh = (_silu(h1) * h3).astype(BF16)
        y_ref[...] = _dot(hh, w2_ref[0].astype(BF16)) * g_ref[...]


def _experts(xg, row_gate, blk_e, n_used, w1, w3, w2, tm):
    rows, d = xg.shape
    de = w1.shape[2]
    grid_spec = pltpu.PrefetchScalarGridSpec(
        num_scalar_prefetch=2, grid=(rows // tm,),
        in_specs=[pl.BlockSpec((tm, d), lambda i, be, nu: (i, 0)),
                  pl.BlockSpec((1, d, de), lambda i, be, nu: (be[i], 0, 0)),
                  pl.BlockSpec((1, d, de), lambda i, be, nu: (be[i], 0, 0)),
                  pl.BlockSpec((1, de, d), lambda i, be, nu: (be[i], 0, 0)),
                  pl.BlockSpec((tm, 1), lambda i, be, nu: (i, 0))],
        out_specs=pl.BlockSpec((tm, d), lambda i, be, nu: (i, 0)))
    return pl.pallas_call(
        _experts_kernel, grid_spec=grid_spec, out_shape=jax.ShapeDtypeStruct((rows, d), F32),
        compiler_params=_params("arbitrary"), name="experts",
    )(blk_e, n_used, xg, w1, w3, w2, row_gate)


def _moe(hmod, logits, n_groups, n_experts, w1, w3, w2, tm):
    n_tok, d = hmod.shape
    epg = n_experts // n_groups
    lg = logits[:, :n_groups]
    grp = jnp.argmax(lg, axis=-1)
    p_grp = jnp.take_along_axis(jax.nn.softmax(lg, -1), grp[:, None], -1)
    le = logits[:, n_groups:n_groups + n_experts].reshape(n_tok, n_groups, epg)
    le = jnp.take_along_axis(le, grp[:, None, None], axis=1)[:, 0]
    top_v, top_i = lax.top_k(le, TOP_K)
    gate = (p_grp * jax.nn.softmax(top_v, -1)).reshape(-1)
    expert = (grp[:, None] * epg + top_i).reshape(-1).astype(jnp.int32)
    onehot = (expert[:, None] == jnp.arange(n_experts, dtype=jnp.int32)[None, :]).astype(jnp.int32)
    rank = jnp.sum((jnp.cumsum(onehot, axis=0) - onehot) * onehot, axis=1)
    counts = jnp.sum(onehot, axis=0)
    padded = (counts + tm - 1) // tm * tm
    pend = jnp.cumsum(padded)
    dest = (pend - padded)[expert] + rank
    n_blk = (n_tok * TOP_K + tm - 1) // tm + n_experts
    blk_e = jnp.minimum(jnp.sum(pend[None, :] <= (jnp.arange(n_blk) * tm)[:, None], axis=1), n_experts - 1)
    n_used = (pend[-1] // tm).reshape(1)
    src = jnp.zeros((n_blk * tm,), jnp.int32).at[dest].set(jnp.arange(n_tok * TOP_K, dtype=jnp.int32) // TOP_K)
    row_gate = jnp.zeros((n_blk * tm,), F32).at[dest].set(gate)
    y = _experts(hmod[src], row_gate[:, None], blk_e.astype(jnp.int32), n_used.astype(jnp.int32), w1, w3, w2, tm)
    dest = dest.reshape(n_tok, TOP_K)
    return y[dest[:, 0]] + y[dest[:, 1]]


def _rope_table(n_lat, n_ctx):
    rows = n_lat // GRID_W
    row = jnp.broadcast_to(jnp.arange(rows)[:, None], (rows, GRID_W)).reshape(-1)
    col = jnp.broadcast_to(jnp.arange(GRID_W)[None, :], (rows, GRID_W)).reshape(-1)
    pos = jnp.stack([row, col], -1).astype(F32)
    inv = ROPE_BASE ** (-jnp.arange(ROPE_FREQS, dtype=F32) / ROPE_FREQS)
    ang = pos[:, :, None] * inv
    ang = jnp.broadcast_to(ang[:, :, None, :], (n_lat, 2, 2, ROPE_FREQS)).reshape(n_lat, MLA_ROPE)
    lat = jnp.concatenate([jnp.cos(ang), jnp.sin(ang)], axis=-1)
    ctx = jnp.concatenate([jnp.ones((n_ctx, MLA_ROPE), F32), jnp.zeros((n_ctx, MLA_ROPE), F32)], axis=-1)
    return jnp.concatenate([lat, ctx], axis=0)


def _rot_cols(w):
    wr = w.reshape(w.shape[:-1] + (2, 2, ROPE_FREQS))
    return jnp.stack([-wr[..., 1, :], wr[..., 0, :]], axis=-2).reshape(w.shape)


def kernel(x, c, ctx, c_ctx, ada_w, ada_b, ln_mix_g, ln_mix_b, ln_ffn_g, ln_ffn_b, ev_w_in, ev_conv_qkv, ev_a_log,
           ev_dt_bias, ev_o_norm, ev_conv_x_w, ev_conv_x_b, ev_w_r, ev_b_r, ev_w_i, ev_b_i, ev_lam, ev_w_out,
           od_w_in, od_kv_norm, od_w_ukv, od_w_out, moe_w_grp, moe_b_grp, moe_w_exp, moe_b_exp, moe_w1, moe_w3,
           moe_w2):
    bsz, n_lat, d = x.shape
    n_ctx = ctx.shape[1]
    s = n_lat + n_ctx
    depth = ada_w.shape[0]
    alpha = (2.0 * depth) ** 0.25
    r = min(256, n_ctx)
    assert n_lat % r == 0 and n_ctx % r == 0 and n_lat % GRID_W == 0 and bsz + 1 <= 8
    dn_heads = ev_a_log.shape[-1]
    vw = dn_heads * DN_DV
    width = ev_lam.shape[-1]
    lru_blocks = ev_w_r.shape[2]
    assert width // lru_blocks == LANES
    rank = od_kv_norm.shape[-1]
    mla_heads = od_w_ukv.shape[-1] // (MLA_NOPE + MLA_V)
    n_groups, n_experts = moe_w_grp.shape[-1], moe_w_exp.shape[-1]
    moe_tm = 256 if (bsz * s * TOP_K) // n_experts >= 512 else 128

    xs = jnp.concatenate([x, ctx], axis=1)
    c_all = jnp.zeros((8, d), F32).at[:bsz].set(c).at[bsz].set(c_ctx)
    mods = _adaln(c_all, ada_w, ada_b).reshape(depth, 8, 6, d)

    def seg_tab(layer, k):
        lat = mods[layer, :bsz, k]
        ctx_v = jnp.broadcast_to(mods[layer, bsz, k][None], (bsz, d))
        return jnp.stack([lat, ctx_v], axis=1)[:, :, None, :]

    rope_tab = _rope_table(n_lat, n_ctx)
    hmod = _modulate(xs, seg_tab(0, 0), seg_tab(0, 1), r, n_lat)
    for layer in range(depth):
        i = layer // 2
        last = layer == depth - 1
        if layer % 2 == 0:
            qkvw = 3 * vw
            w = ev_w_in[i]
            o_z, o_a, o_b, o_x, o_y = qkvw, qkvw + vw, qkvw + vw + 2 * dn_heads, qkvw + vw + 4 * dn_heads, \
                qkvw + vw + 4 * dn_heads + width
            w_perm = jnp.concatenate([w[:, :o_a], w[:, o_x:], w[:, o_a:o_x],
                                      jnp.zeros((d, LANES - 4 * dn_heads), F32)], axis=1).astype(BF16)
            p = _mm(hmod.reshape(bsz * s, d), w_perm).reshape(bsz, s, -1)
            q, k, v, u, gb = _even_prep(p, ev_conv_qkv[i], ev_conv_x_w[i], ev_conv_x_b[i], -jnp.exp(ev_a_log[i]),
                                        ev_dt_bias[i], r, n_lat, dn_heads, width)
            hb = dn_heads if dn_heads <= 4 else 4
            groups = dn_heads // hb
            gsm = gb[:, :, :4 * dn_heads].reshape(bsz, s, 2, 2, groups, hb)
            gsm = gsm.transpose(0, 4, 1, 2, 3, 5).reshape(bsz, groups, s, 4 * hb)
            gates = jnp.pad(gsm, ((0, 0), (0, 0), (0, 0), (0, LANES - 4 * hb)))
            o_f, o_b = _delta(q, k, v, gates, n_lat, dn_heads, hb)
            w_ri = jnp.concatenate([ev_w_r[i], ev_w_i[i]], axis=-1).astype(BF16)
            b_ri = jnp.concatenate([ev_b_r[i].reshape(2, lru_blocks, 1, LANES),
                                    ev_b_i[i].reshape(2, lru_blocks, 1, LANES)], axis=-1)
            spl = (LRU_C * jax.nn.softplus(-ev_lam[i])).reshape(2, lru_blocks, 1, LANES)
            h_f, h_b = _lru(u, w_ri, b_ri, spl, r, n_lat)
            mix = _even_out(o_f, o_b, p, h_f, h_b, ev_o_norm[i], r, dn_heads)
            w_out = ev_w_out[i]
        else:
            w = od_w_in[i]
            nq = mla_heads * MLA_QD
            wq = w[:, :nq].reshape(d, mla_heads, MLA_QD)
            wq = jnp.concatenate([wq, _rot_cols(wq[..., MLA_NOPE:])], axis=-1).transpose(1, 0, 2).astype(BF16)
            w_kr = w[:, nq + rank:]
            w_c = jnp.concatenate([w[:, nq:nq + rank], w_kr, _rot_cols(w_kr)], axis=-1).astype(BF16)
            w_u = od_w_ukv[i].reshape(rank, mla_heads, MLA_NOPE + MLA_V).transpose(1, 0, 2).astype(BF16)
            qh = _q_proj(hmod, wq, rope_tab, mla_heads)
            ckv, kr = _ckv_proj(hmod, w_c, rope_tab, od_kv_norm[i])
            kh, vh = _kv_up(ckv, w_u, kr, mla_heads)
            mix = jnp.concatenate([_flash(qh, kh, vh, (0, n_lat), (0, s)),
                                   _flash(qh, kh, vh, (n_lat, n_ctx), (n_lat, n_ctx))], axis=1)
            w_out = od_w_out[i]
        y = _mm(mix.reshape(bsz * s, -1), w_out.astype(BF16)).reshape(bsz, s, d)
        nr = LANES * ((n_groups + n_experts + LANES - 1) // LANES)
        w_rt = jnp.pad(jnp.concatenate([moe_w_grp[layer], moe_w_exp[layer]], axis=1),
                       ((0, 0), (0, nr - n_groups - n_experts)))
        b_rt = jnp.pad(jnp.concatenate([moe_b_grp[layer], moe_b_exp[layer]]), (0, nr - n_groups - n_experts))
        xs, hmod, logits = _postnorm(xs, y, seg_tab(layer, 2), ln_mix_g[layer], ln_mix_b[layer], r, n_lat, alpha,
                                     shift=seg_tab(layer, 3), scale=seg_tab(layer, 4), router=(w_rt, b_rt[None]))
        f = _moe(hmod.reshape(bsz * s, d), logits.reshape(bsz * s, nr), n_groups, n_experts,
                 moe_w1[layer], moe_w3[layer], moe_w2[layer], moe_tm).reshape(bsz, s, d)
        if last:
            (xs,) = _postnorm(xs, f, seg_tab(layer, 5), ln_ffn_g[layer], ln_ffn_b[layer], r, n_lat, alpha,
                              rows=n_lat)
        else:
            xs, hmod = _postnorm(xs, f, seg_tab(layer, 5), ln_ffn_g[layer], ln_ffn_b[layer], r, n_lat, alpha,
                                 shift=seg_tab(layer + 1, 0), scale=seg_tab(layer + 1, 1))
    return xs
```

```python
import functools
import math

import jax
import jax.numpy as jnp
from jax import lax
from jax.experimental import pallas as pl
from jax.experimental.pallas import tpu as pltpu

F32 = jnp.float32
BF16 = jnp.bfloat16
HIGHEST = lax.Precision.HIGHEST

LANES = 128
DN_DK = 128
DN_DV = 128
DN_CHUNK = 64
MLA_NOPE = 128
MLA_ROPE = 64
MLA_V = 128
MLA_QD = MLA_NOPE + MLA_ROPE
GRID_W = 64
ROPE_FREQS = MLA_ROPE // 4
ROPE_BASE = 10000.0
LRU_C = 8.0
LN_EPS = 1e-5
NORM_EPS = 1e-6
TOP_K = 2
FLASH_SUB = 256
VMEM_LIMIT = 48 * 1024 * 1024


def _pick(n, cands):
    for c in cands:
        if n % c == 0:
            return c
    raise ValueError(f"no tile for {n} in {cands}")


def _params(*sem):
    return pltpu.CompilerParams(dimension_semantics=sem, vmem_limit_bytes=VMEM_LIMIT)


def _dot(a, b, precision=None):
    return jnp.dot(a, b, preferred_element_type=F32, precision=precision)


def _dot_nt(a, b, precision=None):
    return lax.dot_general(a, b, (((1,), (1,)), ((), ())), preferred_element_type=F32, precision=precision)


def _sigmoid(x):
    return 1.0 / (1.0 + jnp.exp(-x))


def _silu(x):
    return x * _sigmoid(x)


def _softplus(x):
    return jnp.maximum(x, 0.0) + jnp.log1p(jnp.exp(-jnp.abs(x)))


def _expm1_nonpos(x):
    u = jnp.exp(x)
    safe = (x > -0.5) & (u < 1.0)
    stable = (u - 1.0) * x / jnp.log(jnp.where(safe, u, 0.5))
    return jnp.where(safe, stable, jnp.where(u < 1.0, u - 1.0, x))


def _adaln_kernel(c_ref, w_ref, b_ref, o_ref):
    a = _silu(c_ref[...])
    o_ref[0] = _dot(a.astype(BF16), w_ref[0].astype(BF16)) + b_ref[0]


def _adaln(c_all, ada_w, ada_b):
    n_layer, d, n6 = ada_w.shape
    tn = _pick(n6, (1024, 512, 256, 128))
    return pl.pallas_call(
        _adaln_kernel, grid=(n_layer, n6 // tn),
        in_specs=[pl.BlockSpec((8, d), lambda l, j: (0, 0)),
                  pl.BlockSpec((1, d, tn), lambda l, j: (l, 0, j)),
                  pl.BlockSpec((1, 1, tn), lambda l, j: (l, 0, j))],
        out_specs=pl.BlockSpec((1, 8, tn), lambda l, j: (l, 0, j)),
        out_shape=jax.ShapeDtypeStruct((n_layer, 8, n6), F32),
        compiler_params=_params("arbitrary", "arbitrary"), name="adaln",
    )(c_all, ada_w, ada_b.reshape(n_layer, 1, n6))


def _seg_spec(d, n_lat_blocks):
    return pl.BlockSpec((1, 1, 1, d), lambda b, i: (b, jnp.where(i >= n_lat_blocks, 1, 0), 0, 0))


def _modulate_kernel(x_ref, sh_ref, sc_ref, h_ref):
    h_ref[0] = (x_ref[0] * (1.0 + sc_ref[0, 0]) + sh_ref[0, 0]).astype(BF16)


def _modulate(x, shift, scale, r, n_lat):
    b, s, d = x.shape
    row = pl.BlockSpec((1, r, d), lambda b_, i: (b_, i, 0))
    return pl.pallas_call(
        _modulate_kernel, grid=(b, s // r),
        in_specs=[row, _seg_spec(d, n_lat // r), _seg_spec(d, n_lat // r)],
        out_specs=row, out_shape=jax.ShapeDtypeStruct((b, s, d), BF16),
        compiler_params=_params("arbitrary", "arbitrary"), name="modulate",
    )(x, shift, scale)


def _postnorm_kernel(*refs, alpha, n_y, row_gated, with_h, with_router):
    x_ref = refs[0]
    pos = 1
    y = None
    for _ in range(n_y):
        term = refs[pos][0].astype(F32)
        pos += 1
        if row_gated:
            term = term * refs[pos][0]
            pos += 1
        y = term if y is None else y + term
    gate_ref, g_ref, b_ref = refs[pos:pos + 3]
    pos += 3
    if with_h:
        sh_ref, sc_ref = refs[pos:pos + 2]
        pos += 2
    if with_router:
        wr_ref, br_ref = refs[pos:pos + 2]
        pos += 2
    xo_ref = refs[pos]
    v = alpha * x_ref[0] + gate_ref[0, 0] * y
    mu = jnp.mean(v, axis=-1, keepdims=True)
    vc = v - mu
    var = jnp.mean(vc * vc, axis=-1, keepdims=True)
    xn = vc * lax.rsqrt(var + LN_EPS) * g_ref[...] + b_ref[...]
    xo_ref[0] = xn
    if with_h:
        h = xn * (1.0 + sc_ref[0, 0]) + sh_ref[0, 0]
        refs[pos + 1][0] = h.astype(refs[pos + 1].dtype)
        if with_router:
            refs[pos + 2][0] = _dot(h, wr_ref[...], HIGHEST) + br_ref[...]


def _postnorm(x, ys, gate, ln_g, ln_b, r, n_lat, alpha, shift=None, scale=None, router=None, rows=None,
              row_gates=None, h_dtype=BF16):
    b, s, d = x.shape
    rows = s if rows is None else rows
    nlb = n_lat // r
    row = pl.BlockSpec((1, r, d), lambda b_, i: (b_, i, 0))
    col1 = pl.BlockSpec((1, r, 1), lambda b_, i: (b_, i, 0))
    vec = pl.BlockSpec((1, d), lambda b_, i: (0, 0))
    args, in_specs = [x], [row]
    for j, y in enumerate(ys):
        args.append(y)
        in_specs.append(row)
        if row_gates is not None:
            args.append(row_gates[j])
            in_specs.append(col1)
    args += [gate, ln_g.reshape(1, d), ln_b.reshape(1, d)]
    in_specs += [_seg_spec(d, nlb), vec, vec]
    out_shape = [jax.ShapeDtypeStruct((b, rows, d), F32)]
    out_specs = [row]
    with_h = shift is not None
    if with_h:
        args += [shift, scale]
        in_specs += [_seg_spec(d, nlb), _seg_spec(d, nlb)]
        out_shape.append(jax.ShapeDtypeStruct((b, rows, d), h_dtype))
        out_specs.append(row)
    if router is not None:
        w_r, b_r = router
        nr = w_r.shape[1]
        args += [w_r, b_r]
        in_specs += [pl.BlockSpec((d, nr), lambda b_, i: (0, 0)), pl.BlockSpec((1, nr), lambda b_, i: (0, 0))]
        out_shape.append(jax.ShapeDtypeStruct((b, rows, nr), F32))
        out_specs.append(pl.BlockSpec((1, r, nr), lambda b_, i: (b_, i, 0)))
    return pl.pallas_call(
        functools.partial(_postnorm_kernel, alpha=alpha, n_y=len(ys), row_gated=row_gates is not None,
                          with_h=with_h, with_router=router is not None),
        grid=(b, rows // r), in_specs=in_specs, out_specs=out_specs, out_shape=out_shape,
        compiler_params=_params("arbitrary", "arbitrary"), name="postnorm",
    )(*args)


def _mm_kernel(a_ref, b_ref, o_ref):
    o_ref[...] = _dot(a_ref[...].astype(BF16), b_ref[...].astype(BF16)).astype(o_ref.dtype)


def _mm(a, b, out_dtype=F32):
    m, k = a.shape
    n = b.shape[1]
    tm = _pick(m, (512, 256, 128, 64, 8))
    tn = _pick(n, (1024, 896, 768, 640, 512, 384, 256, 128))
    return pl.pallas_call(
        _mm_kernel, grid=(m // tm, n // tn),
        in_specs=[pl.BlockSpec((tm, k), lambda i, j: (i, 0)), pl.BlockSpec((k, tn), lambda i, j: (0, j))],
        out_specs=pl.BlockSpec((tm, tn), lambda i, j: (i, j)),
        out_shape=jax.ShapeDtypeStruct((m, n), out_dtype),
        compiler_params=_params("arbitrary", "arbitrary"), name="mm",
    )(a, b)


def _even_prep_kernel(qkv_ref, qkv_p_ref, qkv_n_ref, xr_ref, xr_p_ref, xr_n_ref, ab_ref,
                      cw_ref, xw_ref, xb_ref, nea_ref, dtb_ref,
                      q_ref, k_ref, v_ref, u_ref, gb_ref, *, n_lat_blocks, n_blocks, heads, r):
    i = pl.program_id(1)
    pv = jnp.where((i != 0) & (i != n_lat_blocks), 1.0, 0.0)
    nv = jnp.where((i != n_lat_blocks - 1) & (i != n_blocks - 1), 1.0, 0.0)
    row = lax.broadcasted_iota(jnp.int32, (r, 1), 0)

    def conv(x, p8, n8, w):
        p8 = p8 * pv
        n8 = n8 * nv
        xm1 = jnp.where(row == 0, p8[7:8], pltpu.roll(x, 1, 0))
        xm2 = jnp.where(row == 0, p8[6:7], jnp.where(row == 1, p8[7:8], pltpu.roll(x, 2, 0)))
        xp1 = jnp.where(row == r - 1, n8[0:1], pltpu.roll(x, r - 1, 0))
        return w[0:1] * xm2 + w[1:2] * xm1 + w[2:3] * x + w[3:4] * xp1

    for j in range(3 * heads):
        sl = slice(j * LANES, (j + 1) * LANES)
        y = _silu(conv(qkv_ref[0, :, sl], qkv_p_ref[0, :, sl], qkv_n_ref[0, :, sl], cw_ref[:, sl]))
        if j < 2 * heads:
            y = y * lax.rsqrt(jnp.sum(y * y, axis=-1, keepdims=True) + NORM_EPS)
        if j < heads:
            q_ref[0, :, sl] = y * (DN_DK ** -0.5)
        elif j < 2 * heads:
            k_ref[0, :, slice((j - heads) * LANES, (j - heads + 1) * LANES)] = y
        else:
            v_ref[0, :, slice((j - 2 * heads) * LANES, (j - 2 * heads + 1) * LANES)] = y
    for j in range(xr_ref.shape[2] // LANES):
        sl = slice(j * LANES, (j + 1) * LANES)
        u_ref[0, :, sl] = conv(xr_ref[0, :, sl], xr_p_ref[0, :, sl], xr_n_ref[0, :, sl], xw_ref[:, sl]) + xb_ref[:, sl]
    ab = ab_ref[0]
    lane = lax.broadcasted_iota(jnp.int32, ab.shape, 1)
    g = nea_ref[...] * _softplus(ab + dtb_ref[...])
    gb_ref[0] = jnp.where(lane < 2 * heads, g, _sigmoid(ab))


def _even_prep(p, conv_qkv, conv_x_w, conv_x_b, neg_exp_a, dt_bias, r, n_lat, heads, width):
    b, s, _ = p.shape
    vw = heads * DN_DV
    qkvw = 3 * vw
    assert width == vw and qkvw % width == 0
    nb, nlb, r8 = s // r, n_lat // r, r // 8
    n8 = s // 8

    def cur(wd, cb):
        return pl.BlockSpec((1, r, wd), lambda b_, i: (b_, i, cb))

    def prev(wd, cb):
        return pl.BlockSpec((1, 8, wd), lambda b_, i: (b_, jnp.maximum(i * r8 - 1, 0), cb))

    def nxt(wd, cb):
        return pl.BlockSpec((1, 8, wd), lambda b_, i: (b_, jnp.minimum((i + 1) * r8, n8 - 1), cb))

    def par(shape):
        return pl.BlockSpec(shape, lambda b_, i: (0, 0))

    xcb = (qkvw + vw) // width
    abcb = (qkvw + vw + 2 * width) // LANES
    pad = LANES - 2 * heads
    nea = jnp.pad(neg_exp_a.reshape(1, 2 * heads), ((0, 0), (0, pad)))
    dtb = jnp.pad(dt_bias.reshape(1, 2 * heads), ((0, 0), (0, pad)))
    outs = [jax.ShapeDtypeStruct((b, s, vw), F32)] * 3 + [jax.ShapeDtypeStruct((b, s, width), F32),
                                                            jax.ShapeDtypeStruct((b, s, LANES), F32)]
    return pl.pallas_call(
        functools.partial(_even_prep_kernel, n_lat_blocks=nlb, n_blocks=nb, heads=heads, r=r),
        grid=(b, nb),
        in_specs=[cur(qkvw, 0), prev(qkvw, 0), nxt(qkvw, 0), cur(width, xcb), prev(width, xcb), nxt(width, xcb),
                  cur(LANES, abcb), par((4, qkvw)), par((4, width)), par((1, width)), par((1, LANES)), par((1, LANES))],
        out_specs=[cur(vw, 0), cur(vw, 0), cur(vw, 0), cur(width, 0), cur(LANES, 0)],
        out_shape=outs, compiler_params=_params("arbitrary", "arbitrary"), name="even_prep",
    )(p, p, p, p, p, p, p, conv_qkv, conv_x_w, conv_x_b.reshape(1, width), nea, dtb)


def _split2(x):
    hi = x.astype(BF16)
    return hi, (x - hi.astype(F32)).astype(BF16)


def _dot3(a, b):
    return _dot(a[0], b[0]) + (_dot(a[0], b[1]) + _dot(a[1], b[0]))


def _delta_local_kernel(q_ref, k_ref, v_ref, g_ref, u_ref, wq_ref, ak_ref, gt_ref, *, heads):
    c = DN_CHUNK
    ri = lax.broadcasted_iota(jnp.int32, (c, c), 0)
    ci = lax.broadcasted_iota(jnp.int32, (c, c), 1)
    eye = jnp.where(ri == ci, 1.0, 0.0)
    gall = g_ref[0]
    g1 = gall.astype(BF16)
    rem = gall - g1.astype(F32)
    g2 = rem.astype(BF16)
    g3 = (rem - g2.astype(F32)).astype(BF16)
    for d in range(2):
        incl = (ri >= ci) if d == 0 else (ri <= ci)
        strict = (ri > ci) if d == 0 else (ri < ci)
        last = c - 1 if d == 0 else 0
        levels = [strict & ((ri >> (k + 1)) == (ci >> (k + 1))) & ((ri >> k) != (ci >> k)) for k in range(6)]
        ones = jnp.where(incl, 1.0, 0.0).astype(BF16)
        gcum = _dot(ones, g1) + (_dot(ones, g2) + _dot(ones, g3))
        gcum_t = gcum.T
        gt_ref[d] = jnp.exp(gcum[last:last + 1, :])
        for h in range(heads):
            col = d * heads + h
            sl = slice(h * LANES, (h + 1) * LANES)
            gc = gcum[:, col:col + 1]
            gr = gcum_t[col:col + 1, :]
            g_last = gcum[last:last + 1, col:col + 1]
            beta = gall[:, 2 * heads + col:2 * heads + col + 1]
            decay = jnp.where(incl, jnp.exp(jnp.where(incl, gc - gr, 0.0)), 0.0)
            qh, kh, vh = q_ref[0, :, sl], k_ref[0, :, sl], v_ref[0, :, sl]
            k_beta = kh * beta
            kh16 = kh.astype(BF16)
            a_kk = jnp.where(strict, _dot_nt(k_beta.astype(BF16), kh16) * decay, 0.0)
            t = eye - jnp.where(levels[0], a_kk, 0.0)
            for k in range(1, 6):
                ts = _split2(t)
                x = _dot3(ts, _split2(jnp.where(levels[k], a_kk, 0.0)))
                t = t - _dot3(_split2(x), ts)
            eg = jnp.exp(gc)
            sol = _dot3(_split2(t), _split2(jnp.concatenate([vh * beta, k_beta * eg], axis=1)))
            u_ref[d, h] = sol[:, :DN_DV]
            wq_ref[d, h, 0:c] = sol[:, DN_DV:].astype(BF16)
            wq_ref[d, h, c:2 * c] = (qh * eg).astype(BF16)
            a_qk = jnp.where(incl, _dot_nt(qh.astype(BF16), kh16) * decay, 0.0)
            ak_ref[d, h, 0:c] = a_qk.astype(BF16)
            ak_ref[d, h, c:c + DN_DK] = (kh * jnp.exp(g_last - gc)).T.astype(BF16)


def _delta_scan_kernel(uf, wqf, akf, gtf, ub, wqb, akb, gtb, of, ob, state, *, heads):
    c = DN_CHUNK

    @pl.when(pl.program_id(1) == 0)
    def _():
        state[...] = jnp.zeros_like(state)

    for d, (u_ref, wq_ref, ak_ref, gt_ref, o_ref) in enumerate(((uf, wqf, akf, gtf, of), (ub, wqb, akb, gtb, ob))):
        gt = gt_ref[...]
        for h in range(heads):
            col = d * heads + h
            s_prev = state[d, h]
            r1 = _dot(wq_ref[h], s_prev.astype(BF16))
            v_new = u_ref[h] - r1[:c]
            r2 = _dot(ak_ref[h], v_new.astype(BF16))
            o_ref[0, :, h * LANES:(h + 1) * LANES] = r1[c:] + r2[:c]
            state[d, h] = s_prev * gt[:, col:col + 1] + r2[c:]


def _delta(q, k, v, gates, n_lat, heads):
    b, s, vw = q.shape
    c = DN_CHUNK
    n_chunks, n_lat_c = s // c, n_lat // c
    n_ctx_c = n_chunks - n_lat_c
    seq = pl.BlockSpec((1, c, vw), lambda b_, i: (b_, i, 0))

    def loc(rows, cols):
        return pl.BlockSpec((2, None, None, heads, rows, cols), lambda b_, i: (0, b_, i, 0, 0, 0))

    u, wq, ak, gt = pl.pallas_call(
        functools.partial(_delta_local_kernel, heads=heads), grid=(b, n_chunks),
        in_specs=[seq, seq, seq, pl.BlockSpec((1, c, LANES), lambda b_, i: (b_, i, 0))],
        out_specs=[loc(c, DN_DV), loc(2 * c, DN_DV), loc(c + DN_DK, c),
                   pl.BlockSpec((2, None, None, 1, LANES), lambda b_, i: (0, b_, i, 0, 0))],
        out_shape=[jax.ShapeDtypeStruct((2, b, n_chunks, heads, c, DN_DV), F32),
                   jax.ShapeDtypeStruct((2, b, n_chunks, heads, 2 * c, DN_DV), BF16),
                   jax.ShapeDtypeStruct((2, b, n_chunks, heads, c + DN_DK, c), BF16),
                   jax.ShapeDtypeStruct((2, b, n_chunks, 1, LANES), F32)],
        compiler_params=_params("arbitrary", "arbitrary"), name="delta_local",
    )(q, k, v, gates)

    def fwd(st):
        return jnp.where(st < n_ctx_c, n_lat_c + st, st - n_ctx_c)

    def bwd(st):
        return jnp.where(st < n_ctx_c, n_chunks - 1 - st, n_lat_c - 1 - (st - n_ctx_c))

    def chunk(d, order, rows, cols):
        return pl.BlockSpec((None, None, None, heads, rows, cols), lambda b_, st: (d, b_, order(st), 0, 0, 0))

    def gspec(d, order):
        return pl.BlockSpec((None, None, None, 1, LANES), lambda b_, st: (d, b_, order(st), 0, 0))

    def ospec(order):
        return pl.BlockSpec((1, c, vw), lambda b_, st: (b_, order(st), 0))

    ins, args = [], []
    for d, order in ((0, fwd), (1, bwd)):
        ins += [chunk(d, order, c, DN_DV), chunk(d, order, 2 * c, DN_DV), chunk(d, order, c + DN_DK, c),
                gspec(d, order)]
        args += [u, wq, ak, gt]
    out = jax.ShapeDtypeStruct((b, s, vw), F32)
    return pl.pallas_call(
        functools.partial(_delta_scan_kernel, heads=heads), grid=(b, n_chunks),
        in_specs=ins, out_specs=[ospec(fwd), ospec(bwd)], out_shape=[out, out],
        scratch_shapes=[pltpu.VMEM((2, heads, DN_DK, DN_DV), F32)],
        compiler_params=_params("arbitrary", "arbitrary"), name="delta_scan",
    )(*args)


def _lru_kernel(uf_ref, ub_ref, wri_ref, bri_ref, spl_ref, hf_ref, hb_ref, carry, *, r, n_blk):
    @pl.when(pl.program_id(1) == 0)
    def _():
        carry[...] = jnp.zeros_like(carry)

    row = lax.broadcasted_iota(jnp.int32, (r, 1), 0)
    for d, (u_ref, h_ref) in enumerate(((uf_ref, hf_ref), (ub_ref, hb_ref))):
        for n in range(n_blk):
            sl = slice(n * LANES, (n + 1) * LANES)
            u = u_ref[0, :, sl]
            ri = _dot(u.astype(BF16), wri_ref[d, n]) + bri_ref[d, n]
            rg = _sigmoid(ri[:, :LANES])
            ig = _sigmoid(ri[:, LANES:])
            log_a = -spl_ref[d, n] * rg
            a = jnp.exp(log_a)
            bt = jnp.sqrt(-_expm1_nonpos(2.0 * log_a)) * (ig * u)
            sh = 1
            while sh < r:
                if d == 0:
                    keep = row >= sh
                    a_s, b_s = pltpu.roll(a, sh, 0), pltpu.roll(bt, sh, 0)
                else:
                    keep = row < r - sh
                    a_s, b_s = pltpu.roll(a, r - sh, 0), pltpu.roll(bt, r - sh, 0)
                bt = a * jnp.where(keep, b_s, 0.0) + bt
                a = a * jnp.where(keep, a_s, 1.0)
                sh *= 2
            h = bt + a * carry[d, :, sl]
            h_ref[0, :, sl] = h
            carry[d, :, sl] = h[r - 1:r] if d == 0 else h[0:1]


def _lru(u, w_ri, b_ri, spl, r, n_lat):
    b, s, width = u.shape
    n_blk = width // LANES
    nb, nlb = s // r, n_lat // r
    ncb = nb - nlb

    def fwd(st):
        return jnp.where(st < ncb, nlb + st, st - ncb)

    def bwd(st):
        return jnp.where(st < ncb, nb - 1 - st, nlb - 1 - (st - ncb))

    def seq(order):
        return pl.BlockSpec((1, r, width), lambda b_, st: (b_, order(st), 0))

    def par(shape):
        return pl.BlockSpec(shape, lambda b_, st: (0,) * len(shape))

    out = jax.ShapeDtypeStruct((b, s, width), F32)
    return pl.pallas_call(
        functools.partial(_lru_kernel, r=r, n_blk=n_blk), grid=(b, nb),
        in_specs=[seq(fwd), seq(bwd), par(w_ri.shape), par(b_ri.shape), par(spl.shape)],
        out_specs=[seq(fwd), seq(bwd)], out_shape=[out, out],
        scratch_shapes=[pltpu.VMEM((2, 1, width), F32)],
        compiler_params=_params("arbitrary", "arbitrary"), name="lru",
    )(u, u, w_ri, b_ri, spl)


def _gelu_tanh(x):
    return 0.5 * x * (1.0 + jnp.tanh(math.sqrt(2.0 / math.pi) * (x + 0.044715 * (x * x * x))))


def _even_out_kernel(of_ref, ob_ref, z_ref, hf_ref, hb_ref, y_ref, on_ref, mix_ref, *, heads, n_blk):
    for h in range(heads):
        sl = slice(h * LANES, (h + 1) * LANES)
        o = of_ref[0, :, sl] + ob_ref[0, :, sl]
        o = o * lax.rsqrt(jnp.mean(o * o, axis=-1, keepdims=True) + NORM_EPS) * on_ref[...]
        mix_ref[0, :, sl] = (o * _silu(z_ref[0, :, sl])).astype(BF16)
    for n in range(n_blk):
        sl = slice(n * LANES, (n + 1) * LANES)
        osl = slice((heads + n) * LANES, (heads + n + 1) * LANES)
        mix_ref[0, :, osl] = ((hf_ref[0, :, sl] + hb_ref[0, :, sl]) * _gelu_tanh(y_ref[0, :, sl])).astype(BF16)


def _even_out(o_f, o_b, p, h_f, h_b, o_norm, r, heads):
    b, s, vw = o_f.shape
    width = h_f.shape[2]

    def cur(wd, cb):
        return pl.BlockSpec((1, r, wd), lambda b_, i: (b_, i, cb))

    zcb = (3 * vw) // vw
    ycb = (4 * vw + width) // width
    return pl.pallas_call(
        functools.partial(_even_out_kernel, heads=heads, n_blk=width // LANES), grid=(b, s // r),
        in_specs=[cur(vw, 0), cur(vw, 0), cur(vw, zcb), cur(width, 0), cur(width, 0), cur(width, ycb),
                  pl.BlockSpec((1, DN_DV), lambda b_, i: (0, 0))],
        out_specs=cur(vw + width, 0), out_shape=jax.ShapeDtypeStruct((b, s, vw + width), BF16),
        compiler_params=_params("arbitrary", "arbitrary"), name="even_out",
    )(o_f, o_b, p, h_f, h_b, p, o_norm.reshape(1, DN_DV))


def _rope_pair(y, tab):
    y = y * tab
    return y + pltpu.roll(y, MLA_ROPE, 1)


def _q_proj_kernel(a_ref, w_ref, tab_ref, q_ref, *, scale):
    acc = _dot(a_ref[0], w_ref[0])
    q_ref[0, 0, :, 0:MLA_NOPE] = (acc[:, :MLA_NOPE] * scale).astype(BF16)
    qr = _rope_pair(acc[:, MLA_NOPE:], tab_ref[...])
    q_ref[0, 0, :, MLA_NOPE:MLA_QD] = (qr[:, :MLA_ROPE] * scale).astype(BF16)


def _q_proj(h, w_q, tab, heads):
    b, s, d = h.shape
    tm = _pick(s, (768, 512, 640, 256, 128))
    return pl.pallas_call(
        functools.partial(_q_proj_kernel, scale=math.log2(math.e) * MLA_QD ** -0.5), grid=(b, s // tm, heads),
        in_specs=[pl.BlockSpec((1, tm, d), lambda b_, i, hd: (b_, i, 0)),
                  pl.BlockSpec((1, d, 2 * LANES), lambda b_, i, hd: (hd, 0, 0)),
                  pl.BlockSpec((tm, LANES), lambda b_, i, hd: (i, 0))],
        out_specs=pl.BlockSpec((1, 1, tm, MLA_QD), lambda b_, i, hd: (b_, hd, i, 0)),
        out_shape=jax.ShapeDtypeStruct((b, heads, s, MLA_QD), BF16),
        compiler_params=_params("arbitrary", "arbitrary", "arbitrary"), name="q_proj",
    )(h, w_q, tab)


def _ckv_proj_kernel(a_ref, w_ref, tab_ref, g_ref, ckv_ref, kr_ref, *, rank):
    acc = _dot(a_ref[0], w_ref[...])
    ckv = acc[:, :rank]
    ckv = ckv * lax.rsqrt(jnp.mean(ckv * ckv, axis=-1, keepdims=True) + NORM_EPS) * g_ref[...]
    ckv_ref[0] = ckv.astype(BF16)
    kr_ref[0] = _rope_pair(acc[:, rank:], tab_ref[...]).astype(BF16)


def _ckv_proj(h, w_c, tab, kv_norm):
    b, s, d = h.shape
    rank = kv_norm.shape[0]
    tm = _pick(s, (768, 512, 640, 256, 128))
    return pl.pallas_call(
        functools.partial(_ckv_proj_kernel, rank=rank), grid=(b, s // tm),
        in_specs=[pl.BlockSpec((1, tm, d), lambda b_, i: (b_, i, 0)),
                  pl.BlockSpec((d, rank + LANES), lambda b_, i: (0, 0)),
                  pl.BlockSpec((tm, LANES), lambda b_, i: (i, 0)),
                  pl.BlockSpec((1, rank), lambda b_, i: (0, 0))],
        out_specs=[pl.BlockSpec((1, tm, rank), lambda b_, i: (b_, i, 0)),
                   pl.BlockSpec((1, tm, LANES), lambda b_, i: (b_, i, 0))],
        out_shape=[jax.ShapeDtypeStruct((b, s, rank), BF16), jax.ShapeDtypeStruct((b, s, LANES), BF16)],
        compiler_params=_params("arbitrary", "arbitrary"), name="ckv_proj",
    )(h, w_c, tab, kv_norm.reshape(1, rank))


def _kv_up_kernel(a_ref, w_ref, kr_ref, k_ref, v_ref):
    acc = _dot(a_ref[0], w_ref[0])
    k_ref[0, 0, :, 0:MLA_NOPE] = acc[:, :MLA_NOPE].astype(BF16)
    k_ref[0, 0, :, MLA_NOPE:MLA_QD] = kr_ref[0, :, 0:MLA_ROPE]
    v_ref[0, 0] = acc[:, MLA_NOPE:].astype(BF16)


def _kv_up(ckv, w_ukv, kr, heads):
    b, s, rank = ckv.shape
    tm = _pick(s, (768, 512, 640, 256, 128))
    return pl.pallas_call(
        _kv_up_kernel, grid=(b, s // tm, heads),
        in_specs=[pl.BlockSpec((1, tm, rank), lambda b_, i, hd: (b_, i, 0)),
                  pl.BlockSpec((1, rank, MLA_NOPE + MLA_V), lambda b_, i, hd: (hd, 0, 0)),
                  pl.BlockSpec((1, tm, LANES), lambda b_, i, hd: (b_, i, 0))],
        out_specs=[pl.BlockSpec((1, 1, tm, MLA_QD), lambda b_, i, hd: (b_, hd, i, 0)),
                   pl.BlockSpec((1, 1, tm, MLA_V), lambda b_, i, hd: (b_, hd, i, 0))],
        out_shape=[jax.ShapeDtypeStruct((b, heads, s, MLA_QD), BF16),
                   jax.ShapeDtypeStruct((b, heads, s, MLA_V), BF16)],
        compiler_params=_params("arbitrary", "arbitrary", "arbitrary"), name="kv_up",
    )(ckv, w_ukv, kr)


def _flash_kernel(q_ref, k_ref, v_ref, o_ref, m_ref, l_ref, acc_ref, *, sub):
    j = pl.program_id(3)

    @pl.when(j == 0)
    def _():
        m_ref[...] = jnp.full_like(m_ref, -jnp.inf)
        l_ref[...] = jnp.zeros_like(l_ref)
        acc_ref[...] = jnp.zeros_like(acc_ref)

    k = k_ref[0, 0]
    v = v_ref[0, 0]
    for c in range(q_ref.shape[2] // sub):
        rows = slice(c * sub, (c + 1) * sub)
        s = _dot_nt(q_ref[0, 0, rows, :], k)
        m_prev = m_ref[rows]
        m_new = jnp.maximum(m_prev, jnp.max(s, axis=-1, keepdims=True))
        p = jnp.exp2(s - m_new)
        alpha = jnp.exp2(m_prev - m_new)
        l_ref[rows] = alpha * l_ref[rows] + jnp.sum(p, axis=-1, keepdims=True)
        acc_ref[rows] = alpha * acc_ref[rows] + _dot(p.astype(BF16), v)
        m_ref[rows] = m_new

    @pl.when(j == pl.num_programs(3) - 1)
    def _():
        o_ref[0] = (acc_ref[...] / l_ref[...]).astype(o_ref.dtype)


def _attn_full_kernel(q_ref, k_ref, v_ref, o_ref, *, sub):
    k = k_ref[0, 0]
    v = v_ref[0, 0]
    for c in range(q_ref.shape[2] // sub):
        rows = slice(c * sub, (c + 1) * sub)
        s = _dot_nt(q_ref[0, 0, rows, :], k)
        p = jnp.exp2(s - jnp.max(s, axis=-1, keepdims=True))
        l = jnp.sum(p, axis=-1, keepdims=True)
        o_ref[0, rows, :] = (_dot(p.astype(BF16), v) / l).astype(o_ref.dtype)


def _flash(q, k, v, q_rows, kv_rows):
    b, heads, _, dq = q.shape
    dv = v.shape[3]
    q0, nq = q_rows
    k0, nk = kv_rows
    tq = _pick(math.gcd(nq, q0) if q0 else nq, (1024, 512, 256, 128))
    tk = _pick(math.gcd(nk, k0) if k0 else nk, (nk, 768, 512, 256, 128))
    qo, ko = q0 // tq, k0 // tk
    if nk == tk:
        return pl.pallas_call(
            functools.partial(_attn_full_kernel, sub=min(tq, FLASH_SUB)), grid=(b, heads, nq // tq),
            in_specs=[pl.BlockSpec((1, 1, tq, dq), lambda b_, h, i: (b_, h, qo + i, 0)),
                      pl.BlockSpec((1, 1, tk, dq), lambda b_, h, i: (b_, h, ko, 0)),
                      pl.BlockSpec((1, 1, tk, dv), lambda b_, h, i: (b_, h, ko, 0))],
            out_specs=pl.BlockSpec((1, tq, dv), lambda b_, h, i: (b_, i, h)),
            out_shape=jax.ShapeDtypeStruct((b, nq, heads * dv), BF16),
            compiler_params=_params("arbitrary", "arbitrary", "arbitrary"), name="attn_full",
        )(q, k, v)
    return pl.pallas_call(
        functools.partial(_flash_kernel, sub=min(tq, FLASH_SUB)), grid=(b, heads, nq // tq, nk // tk),
        in_specs=[pl.BlockSpec((1, 1, tq, dq), lambda b_, h, i, j: (b_, h, qo + i, 0)),
                  pl.BlockSpec((1, 1, tk, dq), lambda b_, h, i, j: (b_, h, ko + j, 0)),
                  pl.BlockSpec((1, 1, tk, dv), lambda b_, h, i, j: (b_, h, ko + j, 0))],
        out_specs=pl.BlockSpec((1, tq, dv), lambda b_, h, i, j: (b_, i, h)),
        out_shape=jax.ShapeDtypeStruct((b, nq, heads * dv), BF16),
        scratch_shapes=[pltpu.VMEM((tq, 1), F32), pltpu.VMEM((tq, 1), F32), pltpu.VMEM((tq, dv), F32)],
        compiler_params=_params("arbitrary", "arbitrary", "arbitrary", "arbitrary"), name="flash",
    )(q, k, v)


def _experts_kernel(be_ref, nu_ref, x_ref, w1_ref, w3_ref, w2_ref, y_ref):
    @pl.when(pl.program_id(0) < nu_ref[0])
    def _():
        x = x_ref[...].astype(BF16)
        h1 = _dot(x, w1_ref[0].astype(BF16))
        h3 = _dot(x, w3_ref[0].astype(BF16))
        hh = (_silu(h1) * h3).astype(BF16)
        y_ref[...] = _dot(hh, w2_ref[0].astype(BF16))


def _experts(xg, blk_e, n_used, w1, w3, w2, layer, tm):
    rows, d = xg.shape
    de = w1.shape[3]
    grid_spec = pltpu.PrefetchScalarGridSpec(
        num_scalar_prefetch=2, grid=(rows // tm,),
        in_specs=[pl.BlockSpec((tm, d), lambda i, be, nu: (i, 0)),
                  pl.BlockSpec((None, 1, d, de), lambda i, be, nu: (layer, be[i], 0, 0)),
                  pl.BlockSpec((None, 1, d, de), lambda i, be, nu: (layer, be[i], 0, 0)),
                  pl.BlockSpec((None, 1, de, d), lambda i, be, nu: (layer, be[i], 0, 0))],
        out_specs=pl.BlockSpec((tm, d), lambda i, be, nu: (i, 0)))
    return pl.pallas_call(
        _experts_kernel, grid_spec=grid_spec, out_shape=jax.ShapeDtypeStruct((rows, d), F32),
        compiler_params=_params("arbitrary"), name="experts",
    )(blk_e, n_used, xg, w1, w3, w2)


def _moe(h, logits, n_groups, n_experts, w1, w3, w2, layer, tm):
    n_tok, d = h.shape
    epg = n_experts // n_groups
    lg = logits[:, :n_groups]
    grp = jnp.argmax(lg, axis=-1)
    p_grp = jnp.take_along_axis(jax.nn.softmax(lg, -1), grp[:, None], -1)
    le = logits[:, n_groups:n_groups + n_experts].reshape(n_tok, n_groups, epg)
    le = jnp.take_along_axis(le, grp[:, None, None], axis=1)[:, 0]
    top_v, top_i = lax.top_k(le, TOP_K)
    gate = p_grp * jax.nn.softmax(top_v, -1)
    expert = (grp[:, None] * epg + top_i).reshape(-1).astype(jnp.int32)
    onehot = (expert[:, None] == jnp.arange(n_experts, dtype=jnp.int32)[None, :]).astype(jnp.int32)
    rank = jnp.sum((jnp.cumsum(onehot, axis=0) - onehot) * onehot, axis=1)
    counts = jnp.sum(onehot, axis=0)
    padded = (counts + tm - 1) // tm * tm
    pend = jnp.cumsum(padded)
    dest = (pend - padded)[expert] + rank
    n_blk = (n_tok * TOP_K + tm - 1) // tm + n_experts
    blk_e = jnp.minimum(jnp.sum(pend[None, :] <= (jnp.arange(n_blk) * tm)[:, None], axis=1), n_experts - 1)
    n_used = (pend[-1] // tm).reshape(1)
    src = jnp.zeros((n_blk * tm,), jnp.int32).at[dest].set(jnp.arange(n_tok * TOP_K, dtype=jnp.int32) // TOP_K)
    y = _experts(h[src], blk_e.astype(jnp.int32), n_used.astype(jnp.int32), w1, w3, w2, layer, tm)
    dest = dest.reshape(n_tok, TOP_K)
    return (y[dest[:, 0]], y[dest[:, 1]]), (gate[:, 0:1], gate[:, 1:2])


def _rope_table(n_lat, n_ctx):
    rows = n_lat // GRID_W
    row = jnp.broadcast_to(jnp.arange(rows)[:, None], (rows, GRID_W)).reshape(-1)
    col = jnp.broadcast_to(jnp.arange(GRID_W)[None, :], (rows, GRID_W)).reshape(-1)
    pos = jnp.stack([row, col], -1).astype(F32)
    inv = ROPE_BASE ** (-jnp.arange(ROPE_FREQS, dtype=F32) / ROPE_FREQS)
    ang = pos[:, :, None] * inv
    ang = jnp.broadcast_to(ang[:, :, None, :], (n_lat, 2, 2, ROPE_FREQS)).reshape(n_lat, MLA_ROPE)
    lat = jnp.concatenate([jnp.cos(ang), jnp.sin(ang)], axis=-1)
    ctx = jnp.concatenate([jnp.ones((n_ctx, MLA_ROPE), F32), jnp.zeros((n_ctx, MLA_ROPE), F32)], axis=-1)
    return jnp.concatenate([lat, ctx], axis=0)


def _rot_cols(w):
    wr = w.reshape(w.shape[:-1] + (2, 2, ROPE_FREQS))
    return jnp.stack([-wr[..., 1, :], wr[..., 0, :]], axis=-2).reshape(w.shape)


def kernel(x, c, ctx, c_ctx, ada_w, ada_b, ln_mix_g, ln_mix_b, ln_ffn_g, ln_ffn_b, ev_w_in, ev_conv_qkv, ev_a_log,
           ev_dt_bias, ev_o_norm, ev_conv_x_w, ev_conv_x_b, ev_w_r, ev_b_r, ev_w_i, ev_b_i, ev_lam, ev_w_out,
           od_w_in, od_kv_norm, od_w_ukv, od_w_out, moe_w_grp, moe_b_grp, moe_w_exp, moe_b_exp, moe_w1, moe_w3,
           moe_w2):
    bsz, n_lat, d = x.shape
    n_ctx = ctx.shape[1]
    s = n_lat + n_ctx
    depth = ada_w.shape[0]
    alpha = (2.0 * depth) ** 0.25
    r = min(256, n_ctx)
    assert n_lat % r == 0 and n_ctx % r == 0 and n_lat % GRID_W == 0 and bsz + 1 <= 8
    dn_heads = ev_a_log.shape[-1]
    vw = dn_heads * DN_DV
    width = ev_lam.shape[-1]
    lru_blocks = ev_w_r.shape[2]
    assert width // lru_blocks == LANES
    rank = od_kv_norm.shape[-1]
    mla_heads = od_w_ukv.shape[-1] // (MLA_NOPE + MLA_V)
    n_groups, n_experts = moe_w_grp.shape[-1], moe_w_exp.shape[-1]
    moe_tm = 256 if (bsz * s * TOP_K) // n_experts >= 512 else 128

    xs = jnp.concatenate([x, ctx], axis=1)
    c_all = jnp.zeros((8, d), F32).at[:bsz].set(c).at[bsz].set(c_ctx)
    mods = _adaln(c_all, ada_w, ada_b).reshape(depth, 8, 6, d)

    def seg_tab(layer, k):
        lat = mods[layer, :bsz, k]
        ctx_v = jnp.broadcast_to(mods[layer, bsz, k][None], (bsz, d))
        return jnp.stack([lat, ctx_v], axis=1)[:, :, None, :]

    rope_tab = _rope_table(n_lat, n_ctx)
    hmod = _modulate(xs, seg_tab(0, 0), seg_tab(0, 1), r, n_lat)
    for layer in range(depth):
        i = layer // 2
        last = layer == depth - 1
        if layer % 2 == 0:
            qkvw = 3 * vw
            w = ev_w_in[i]
            o_z, o_a, o_b, o_x, o_y = qkvw, qkvw + vw, qkvw + vw + 2 * dn_heads, qkvw + vw + 4 * dn_heads, \
                qkvw + vw + 4 * dn_heads + width
            w_perm = jnp.concatenate([w[:, :o_a], w[:, o_x:], w[:, o_a:o_x],
                                      jnp.zeros((d, LANES - 4 * dn_heads), F32)], axis=1).astype(BF16)
            p = _mm(hmod.reshape(bsz * s, d), w_perm).reshape(bsz, s, -1)
            q, k, v, u, gb = _even_prep(p, ev_conv_qkv[i], ev_conv_x_w[i], ev_conv_x_b[i], -jnp.exp(ev_a_log[i]),
                                        ev_dt_bias[i], r, n_lat, dn_heads, width)
            o_f, o_b = _delta(q, k, v, gb, n_lat, dn_heads)
            w_ri = jnp.concatenate([ev_w_r[i], ev_w_i[i]], axis=-1).astype(BF16)
            b_ri = jnp.concatenate([ev_b_r[i].reshape(2, lru_blocks, 1, LANES),
                                    ev_b_i[i].reshape(2, lru_blocks, 1, LANES)], axis=-1)
            spl = (LRU_C * jax.nn.softplus(-ev_lam[i])).reshape(2, lru_blocks, 1, LANES)
            h_f, h_b = _lru(u, w_ri, b_ri, spl, r, n_lat)
            mix = _even_out(o_f, o_b, p, h_f, h_b, ev_o_norm[i], r, dn_heads)
            w_out = ev_w_out[i]
        else:
            w = od_w_in[i]
            nq = mla_heads * MLA_QD
            wq = w[:, :nq].reshape(d, mla_heads, MLA_QD)
            wq = jnp.concatenate([wq, _rot_cols(wq[..., MLA_NOPE:])], axis=-1).transpose(1, 0, 2).astype(BF16)
            w_kr = w[:, nq + rank:]
            w_c = jnp.concatenate([w[:, nq:nq + rank], w_kr, _rot_cols(w_kr)], axis=-1).astype(BF16)
            w_u = od_w_ukv[i].reshape(rank, mla_heads, MLA_NOPE + MLA_V).transpose(1, 0, 2).astype(BF16)
            qh = _q_proj(hmod, wq, rope_tab, mla_heads)
            ckv, kr = _ckv_proj(hmod, w_c, rope_tab, od_kv_norm[i])
            kh, vh = _kv_up(ckv, w_u, kr, mla_heads)
            mix = jnp.concatenate([_flash(qh, kh, vh, (0, n_lat), (0, s)),
                                   _flash(qh, kh, vh, (n_lat, n_ctx), (n_lat, n_ctx))], axis=1)
            w_out = od_w_out[i]
        y = _mm(mix.reshape(bsz * s, -1), w_out.astype(BF16)).reshape(bsz, s, d)
        nr = LANES * ((n_groups + n_experts + LANES - 1) // LANES)
        w_rt = jnp.pad(jnp.concatenate([moe_w_grp[layer], moe_w_exp[layer]], axis=1),
                       ((0, 0), (0, nr - n_groups - n_experts)))
        b_rt = jnp.pad(jnp.concatenate([moe_b_grp[layer], moe_b_exp[layer]]), (0, nr - n_groups - n_experts))
        xs, hf, logits = _postnorm(xs, [y], seg_tab(layer, 2), ln_mix_g[layer], ln_mix_b[layer], r, n_lat, alpha,
                                   shift=seg_tab(layer, 3), scale=seg_tab(layer, 4), router=(w_rt, b_rt[None]),
                                   h_dtype=F32)
        ys, gts = _moe(hf.reshape(bsz * s, d), logits.reshape(bsz * s, nr), n_groups, n_experts,
                       moe_w1, moe_w3, moe_w2, layer, moe_tm)
        ys = [t.reshape(bsz, s, d) for t in ys]
        gts = [t.reshape(bsz, s, 1) for t in gts]
        if last:
            (xs,) = _postnorm(xs, ys, seg_tab(layer, 5), ln_ffn_g[layer], ln_ffn_b[layer], r, n_lat, alpha,
                              rows=n_lat, row_gates=gts)
        else:
            xs, hmod = _postnorm(xs, ys, seg_tab(layer, 5), ln_ffn_g[layer], ln_ffn_b[layer], r, n_lat, alpha,
                                 shift=seg_tab(layer + 1, 0), scale=seg_tab(layer + 1, 1), row_gates=gts)
    return xs
```

```python
import functools
import math

import jax
import jax.numpy as jnp
from jax import lax
from jax.experimental import pallas as pl
from jax.experimental.pallas import tpu as pltpu

F32 = jnp.float32
BF16 = jnp.bfloat16
HIGHEST = lax.Precision.HIGHEST

LANES = 128
DN_DK = 128
DN_DV = 128
DN_CHUNK = 64
MLA_NOPE = 128
MLA_ROPE = 64
MLA_V = 128
MLA_QD = MLA_NOPE + MLA_ROPE
GRID_W = 64
ROPE_FREQS = MLA_ROPE // 4
ROPE_BASE = 10000.0
LRU_C = 8.0
LN_EPS = 1e-5
NORM_EPS = 1e-6
TOP_K = 2
FLASH_SUB = 256
VMEM_LIMIT = 48 * 1024 * 1024


def _pick(n, cands):
    for c in cands:
        if n % c == 0:
            return c
    raise ValueError(f"no tile for {n} in {cands}")


def _params(*sem):
    return pltpu.CompilerParams(dimension_semantics=sem, vmem_limit_bytes=VMEM_LIMIT)


def _dot(a, b, precision=None):
    return jnp.dot(a, b, preferred_element_type=F32, precision=precision)


def _dot_nt(a, b, precision=None):
    return lax.dot_general(a, b, (((1,), (1,)), ((), ())), preferred_element_type=F32, precision=precision)


def _sigmoid(x):
    return 1.0 / (1.0 + jnp.exp(-x))


def _silu(x):
    return x * _sigmoid(x)


def _softplus(x):
    return jnp.maximum(x, 0.0) + jnp.log1p(jnp.exp(-jnp.abs(x)))


def _expm1_nonpos(x):
    u = jnp.exp(x)
    safe = (x > -0.5) & (u < 1.0)
    stable = (u - 1.0) * x / jnp.log(jnp.where(safe, u, 0.5))
    return jnp.where(safe, stable, jnp.where(u < 1.0, u - 1.0, x))


def _adaln_kernel(c_ref, w_ref, b_ref, o_ref):
    a = _silu(c_ref[...])
    o_ref[0] = _dot(a.astype(BF16), w_ref[0].astype(BF16)) + b_ref[0]


def _adaln(c_all, ada_w, ada_b):
    n_layer, d, n6 = ada_w.shape
    tn = _pick(n6, (1024, 512, 256, 128))
    return pl.pallas_call(
        _adaln_kernel, grid=(n_layer, n6 // tn),
        in_specs=[pl.BlockSpec((8, d), lambda l, j: (0, 0)),
                  pl.BlockSpec((1, d, tn), lambda l, j: (l, 0, j)),
                  pl.BlockSpec((1, 1, tn), lambda l, j: (l, 0, j))],
        out_specs=pl.BlockSpec((1, 8, tn), lambda l, j: (l, 0, j)),
        out_shape=jax.ShapeDtypeStruct((n_layer, 8, n6), F32),
        compiler_params=_params("arbitrary", "arbitrary"), name="adaln",
    )(c_all, ada_w, ada_b.reshape(n_layer, 1, n6))


def _seg_spec(d, n_lat_blocks):
    return pl.BlockSpec((1, 1, 1, d), lambda b, i: (b, jnp.where(i >= n_lat_blocks, 1, 0), 0, 0))


def _modulate_kernel(x_ref, sh_ref, sc_ref, h_ref):
    h_ref[0] = (x_ref[0] * (1.0 + sc_ref[0, 0]) + sh_ref[0, 0]).astype(BF16)


def _modulate(x, shift, scale, r, n_lat):
    b, s, d = x.shape
    row = pl.BlockSpec((1, r, d), lambda b_, i: (b_, i, 0))
    return pl.pallas_call(
        _modulate_kernel, grid=(b, s // r),
        in_specs=[row, _seg_spec(d, n_lat // r), _seg_spec(d, n_lat // r)],
        out_specs=row, out_shape=jax.ShapeDtypeStruct((b, s, d), BF16),
        compiler_params=_params("arbitrary", "arbitrary"), name="modulate",
    )(x, shift, scale)


def _postnorm_kernel(*refs, alpha, n_y, row_gated, with_h, with_router):
    x_ref = refs[0]
    pos = 1
    y = None
    for _ in range(n_y):
        term = refs[pos][0].astype(F32)
        pos += 1
        if row_gated:
            term = term * refs[pos][0]
            pos += 1
        y = term if y is None else y + term
    gate_ref, g_ref, b_ref = refs[pos:pos + 3]
    pos += 3
    if with_h:
        sh_ref, sc_ref = refs[pos:pos + 2]
        pos += 2
    if with_router:
        wr_ref, br_ref = refs[pos:pos + 2]
        pos += 2
    xo_ref = refs[pos]
    v = alpha * x_ref[0] + gate_ref[0, 0] * y
    mu = jnp.mean(v, axis=-1, keepdims=True)
    vc = v - mu
    var = jnp.mean(vc * vc, axis=-1, keepdims=True)
    xn = vc * lax.rsqrt(var + LN_EPS) * g_ref[...] + b_ref[...]
    xo_ref[0] = xn
    if with_h:
        h = xn * (1.0 + sc_ref[0, 0]) + sh_ref[0, 0]
        refs[pos + 1][0] = h.astype(refs[pos + 1].dtype)
        if with_router:
            refs[pos + 2][0] = _dot(h, wr_ref[...], HIGHEST) + br_ref[...]


def _postnorm(x, ys, gate, ln_g, ln_b, r, n_lat, alpha, shift=None, scale=None, router=None, rows=None,
              row_gates=None, h_dtype=BF16):
    b, s, d = x.shape
    rows = s if rows is None else rows
    nlb = n_lat // r
    row = pl.BlockSpec((1, r, d), lambda b_, i: (b_, i, 0))
    col1 = pl.BlockSpec((1, r, 1), lambda b_, i: (b_, i, 0))
    vec = pl.BlockSpec((1, d), lambda b_, i: (0, 0))
    args, in_specs = [x], [row]
    for j, y in enumerate(ys):
        args.append(y)
        in_specs.append(row)
        if row_gates is not None:
            args.append(row_gates[j])
            in_specs.append(col1)
    args += [gate, ln_g.reshape(1, d), ln_b.reshape(1, d)]
    in_specs += [_seg_spec(d, nlb), vec, vec]
    out_shape = [jax.ShapeDtypeStruct((b, rows, d), F32)]
    out_specs = [row]
    with_h = shift is not None
    if with_h:
        args += [shift, scale]
        in_specs += [_seg_spec(d, nlb), _seg_spec(d, nlb)]
        out_shape.append(jax.ShapeDtypeStruct((b, rows, d), h_dtype))
        out_specs.append(row)
    if router is not None:
        w_r, b_r = router
        nr = w_r.shape[1]
        args += [w_r, b_r]
        in_specs += [pl.BlockSpec((d, nr), lambda b_, i: (0, 0)), pl.BlockSpec((1, nr), lambda b_, i: (0, 0))]
        out_shape.append(jax.ShapeDtypeStruct((b, rows, nr), F32))
        out_specs.append(pl.BlockSpec((1, r, nr), lambda b_, i: (b_, i, 0)))
    return pl.pallas_call(
        functools.partial(_postnorm_kernel, alpha=alpha, n_y=len(ys), row_gated=row_gates is not None,
                          with_h=with_h, with_router=router is not None),
        grid=(b, rows // r), in_specs=in_specs, out_specs=out_specs, out_shape=out_shape,
        compiler_params=_params("arbitrary", "arbitrary"), name="postnorm",
    )(*args)


def _mm_kernel(a_ref, b_ref, o_ref):
    o_ref[...] = _dot(a_ref[...].astype(BF16), b_ref[...].astype(BF16)).astype(o_ref.dtype)


def _mm(a, b, out_dtype=F32):
    m, k = a.shape
    n = b.shape[1]
    tm = _pick(m, (512, 256, 128, 64, 8))
    tn = _pick(n, (1024, 896, 768, 640, 512, 384, 256, 128))
    return pl.pallas_call(
        _mm_kernel, grid=(m // tm, n // tn),
        in_specs=[pl.BlockSpec((tm, k), lambda i, j: (i, 0)), pl.BlockSpec((k, tn), lambda i, j: (0, j))],
        out_specs=pl.BlockSpec((tm, tn), lambda i, j: (i, j)),
        out_shape=jax.ShapeDtypeStruct((m, n), out_dtype),
        compiler_params=_params("arbitrary", "arbitrary"), name="mm",
    )(a, b)


def _even_prep_kernel(qkv_ref, qkv_p_ref, qkv_n_ref, xr_ref, xr_p_ref, xr_n_ref, ab_ref,
                      cw_ref, xw_ref, xb_ref, nea_ref, dtb_ref,
                      q_ref, k_ref, v_ref, u_ref, gb_ref, *, n_lat_blocks, n_blocks, heads, r):
    i = pl.program_id(1)
    pv = jnp.where((i != 0) & (i != n_lat_blocks), 1.0, 0.0)
    nv = jnp.where((i != n_lat_blocks - 1) & (i != n_blocks - 1), 1.0, 0.0)
    row = lax.broadcasted_iota(jnp.int32, (r, 1), 0)

    def conv(x, p8, n8, w):
        p8 = p8 * pv
        n8 = n8 * nv
        xm1 = jnp.where(row == 0, p8[7:8], pltpu.roll(x, 1, 0))
        xm2 = jnp.where(row == 0, p8[6:7], jnp.where(row == 1, p8[7:8], pltpu.roll(x, 2, 0)))
        xp1 = jnp.where(row == r - 1, n8[0:1], pltpu.roll(x, r - 1, 0))
        return w[0:1] * xm2 + w[1:2] * xm1 + w[2:3] * x + w[3:4] * xp1

    for j in range(3 * heads):
        sl = slice(j * LANES, (j + 1) * LANES)
        y = _silu(conv(qkv_ref[0, :, sl], qkv_p_ref[0, :, sl], qkv_n_ref[0, :, sl], cw_ref[:, sl]))
        if j < 2 * heads:
            y = y * lax.rsqrt(jnp.sum(y * y, axis=-1, keepdims=True) + NORM_EPS)
        if j < heads:
            q_ref[0, :, sl] = y * (DN_DK ** -0.5)
        elif j < 2 * heads:
            k_ref[0, :, slice((j - heads) * LANES, (j - heads + 1) * LANES)] = y
        else:
            v_ref[0, :, slice((j - 2 * heads) * LANES, (j - 2 * heads + 1) * LANES)] = y
    for j in range(xr_ref.shape[2] // LANES):
        sl = slice(j * LANES, (j + 1) * LANES)
        u_ref[0, :, sl] = conv(xr_ref[0, :, sl], xr_p_ref[0, :, sl], xr_n_ref[0, :, sl], xw_ref[:, sl]) + xb_ref[:, sl]
    ab = ab_ref[0]
    lane = lax.broadcasted_iota(jnp.int32, ab.shape, 1)
    g = nea_ref[...] * _softplus(ab + dtb_ref[...])
    gb_ref[0] = jnp.where(lane < 2 * heads, g, _sigmoid(ab))


def _even_prep(p, conv_qkv, conv_x_w, conv_x_b, neg_exp_a, dt_bias, r, n_lat, heads, width):
    b, s, _ = p.shape
    vw = heads * DN_DV
    qkvw = 3 * vw
    assert width == vw and qkvw % width == 0
    nb, nlb, r8 = s // r, n_lat // r, r // 8
    n8 = s // 8

    def cur(wd, cb):
        return pl.BlockSpec((1, r, wd), lambda b_, i: (b_, i, cb))

    def prev(wd, cb):
        return pl.BlockSpec((1, 8, wd), lambda b_, i: (b_, jnp.maximum(i * r8 - 1, 0), cb))

    def nxt(wd, cb):
        return pl.BlockSpec((1, 8, wd), lambda b_, i: (b_, jnp.minimum((i + 1) * r8, n8 - 1), cb))

    def par(shape):
        return pl.BlockSpec(shape, lambda b_, i: (0, 0))

    xcb = (qkvw + vw) // width
    abcb = (qkvw + vw + 2 * width) // LANES
    pad = LANES - 2 * heads
    nea = jnp.pad(neg_exp_a.reshape(1, 2 * heads), ((0, 0), (0, pad)))
    dtb = jnp.pad(dt_bias.reshape(1, 2 * heads), ((0, 0), (0, pad)))
    outs = [jax.ShapeDtypeStruct((b, s, vw), F32)] * 3 + [jax.ShapeDtypeStruct((b, s, width), F32),
                                                            jax.ShapeDtypeStruct((b, s, LANES), F32)]
    return pl.pallas_call(
        functools.partial(_even_prep_kernel, n_lat_blocks=nlb, n_blocks=nb, heads=heads, r=r),
        grid=(b, nb),
        in_specs=[cur(qkvw, 0), prev(qkvw, 0), nxt(qkvw, 0), cur(width, xcb), prev(width, xcb), nxt(width, xcb),
                  cur(LANES, abcb), par((4, qkvw)), par((4, width)), par((1, width)), par((1, LANES)), par((1, LANES))],
        out_specs=[cur(vw, 0), cur(vw, 0), cur(vw, 0), cur(width, 0), cur(LANES, 0)],
        out_shape=outs, compiler_params=_params("arbitrary", "arbitrary"), name="even_prep",
    )(p, p, p, p, p, p, p, conv_qkv, conv_x_w, conv_x_b.reshape(1, width), nea, dtb)


def _split2(x):
    hi = x.astype(BF16)
    return hi, (x - hi.astype(F32)).astype(BF16)


DN_CHUNKS_PER_STEP = 1
DN_PACK = 4


def _block_diag(x, row_blk, lane_blk, pack):
    tiled = jnp.concatenate([x] * pack, axis=0)
    return jnp.where(row_blk == lane_blk, tiled, 0.0).astype(BF16)


def _dot_hl(a, b_bd):
    n = a[0].shape[0]
    r = _dot(jnp.concatenate([a[0], a[1]], axis=0), b_bd)
    return r[:n] + r[n:]


def _delta_local_kernel(q_ref, k_ref, v_ref, g_ref, u_ref, wq_ref, ak_ref, gt_ref, *, heads, pack, cpb):
    c = DN_CHUNK
    wa = pack * c
    wk = pack * LANES
    ri = lax.broadcasted_iota(jnp.int32, (c, wa), 0)
    la = lax.broadcasted_iota(jnp.int32, (c, wa), 1)
    ci = la & (c - 1)
    blk_a = la >> 6
    blk_k = lax.broadcasted_iota(jnp.int32, (c, wk), 1) >> 7
    rb_a = lax.broadcasted_iota(jnp.int32, (wa, wa), 0) >> 6
    lb_a = lax.broadcasted_iota(jnp.int32, (wa, wa), 1) >> 6
    rb_k = lax.broadcasted_iota(jnp.int32, (wa, wk), 0) >> 6
    lb_k = lax.broadcasted_iota(jnp.int32, (wa, wk), 1) >> 7
    eye = jnp.where(ri == ci, 1.0, 0.0)
    ri1 = lax.broadcasted_iota(jnp.int32, (c, c), 0)
    ci1 = lax.broadcasted_iota(jnp.int32, (c, c), 1)

    def bd_a(x):
        return _block_diag(x, rb_a, lb_a, pack)

    def bd_k(x):
        return _block_diag(x, rb_k, lb_k, pack)

    def per_head(cols, blk):
        out = cols[0]
        for t in range(1, pack):
            out = jnp.where(blk == t, cols[t], out)
        return out

    incl = [(ri >= ci), (ri <= ci)]
    strict = [(ri > ci), (ri < ci)]
    last = [c - 1, 0]
    levels = [[strict[d] & ((ri >> (k + 1)) == (ci >> (k + 1))) & ((ri >> k) != (ci >> k)) for k in range(6)]
              for d in range(2)]
    ones = [jnp.where(ri1 >= ci1, 1.0, 0.0).astype(BF16), jnp.where(ri1 <= ci1, 1.0, 0.0).astype(BF16)]

    packs = []
    for j in range(cpb):
        rows = slice(j * c, (j + 1) * c)
        gall = g_ref[0, rows]
        g1 = gall.astype(BF16)
        rem = gall - g1.astype(F32)
        g2 = rem.astype(BF16)
        g3 = (rem - g2.astype(F32)).astype(BF16)
        for d in range(2):
            gcum = _dot(ones[d], g1) + (_dot(ones[d], g2) + _dot(ones[d], g3))
            gt_ref[d, j] = jnp.exp(gcum[last[d]:last[d] + 1, :])
            for h0 in range(0, heads, pack):
                cols = [d * heads + h0 + t for t in range(pack)]
                gcs = [gcum[:, cc:cc + 1] for cc in cols]
                gc_a = per_head([jnp.broadcast_to(x, (c, wa)) for x in gcs], blk_a)
                gr_a = jnp.sum(jnp.where(ri == ci, gc_a, 0.0), axis=0, keepdims=True)
                decay = jnp.where(incl[d], jnp.exp(jnp.where(incl[d], gc_a - gr_a, 0.0)), 0.0)
                beta_k = per_head([jnp.broadcast_to(gall[:, 2 * heads + cc:2 * heads + cc + 1], (c, wk))
                                   for cc in cols], blk_k)
                gc_k = per_head([jnp.broadcast_to(x, (c, wk)) for x in gcs], blk_k)
                gl_k = per_head([jnp.broadcast_to(x[last[d]:last[d] + 1], (c, wk)) for x in gcs], blk_k)
                sl = slice(h0 * LANES, (h0 + pack) * LANES)
                q_k, k_k, v_k = q_ref[0, rows, sl], k_ref[0, rows, sl], v_ref[0, rows, sl]
                kb_k = k_k * beta_k
                eg_k = jnp.exp(gc_k)
                packs.append(dict(j=j, d=d, h0=h0, decay=decay, kb=kb_k, q=q_k, k=k_k, eg=eg_k,
                                  rhs_u=v_k * beta_k, rhs_w=kb_k * eg_k, kd=k_k * jnp.exp(gl_k - gc_k)))

    for p in packs:
        lhs = jnp.concatenate([p["kb"], p["q"]], axis=0).astype(BF16)
        p["aa"] = _dot_nt(lhs, bd_k(p["k"]))
    for p in packs:
        d = p["d"]
        p["a"] = jnp.where(strict[d], p["aa"][:c] * p["decay"], 0.0).astype(BF16).astype(F32)
        p["a_qk"] = jnp.where(incl[d], p["aa"][c:] * p["decay"], 0.0)
        p["t"] = eye - jnp.where(levels[d][0], p["a"], 0.0)
    for k in range(1, 6):
        for p in packs:
            p["ts"] = _split2(p["t"])
            p["x"] = _dot_hl(p["ts"], bd_a(jnp.where(levels[p["d"]][k], p["a"], 0.0)))
        for p in packs:
            xs = _split2(p["x"])
            y = _dot_hl(xs, bd_a(p["ts"][0].astype(F32))) + _dot(xs[0], bd_a(p["ts"][1].astype(F32)))
            p["t"] = p["t"] - y
    for p in packs:
        p["ts"] = _split2(p["t"])
    for name in ("u", "w"):
        for p in packs:
            rh, rl = _split2(p["rhs_" + name])
            p[name] = _dot_hl(p["ts"], bd_k(rh.astype(F32))) + _dot(p["ts"][0], bd_k(rl.astype(F32)))
    for p in packs:
        j, d = p["j"], p["d"]
        qd_k = p["q"] * p["eg"]
        for t in range(pack):
            h = p["h0"] + t
            ks = slice(t * LANES, (t + 1) * LANES)
            u_ref[d, j, h] = p["u"][:, ks]
            wq_ref[d, j, h, 0:c] = p["w"][:, ks].astype(BF16)
            wq_ref[d, j, h, c:2 * c] = qd_k[:, ks].astype(BF16)
            ak_ref[d, j, h, 0:c] = p["a_qk"][:, t * c:(t + 1) * c].astype(BF16)
            ak_ref[d, j, h, c:c + DN_DK] = p["kd"][:, ks].T.astype(BF16)


def _delta_scan_kernel(uf, wqf, akf, gtf, ub, wqb, akb, gtb, of, ob, state, *, heads):
    c = DN_CHUNK

    @pl.when(pl.program_id(1) == 0)
    def _():
        state[...] = jnp.zeros_like(state)

    dirs = ((uf, wqf, akf, gtf, of), (ub, wqb, akb, gtb, ob))
    chains = [(d, h) for d in range(2) for h in range(heads)]
    s_prev = {ch: state[ch[0], ch[1]] for ch in chains}
    r1 = {(d, h): _dot(dirs[d][1][h], s_prev[(d, h)].astype(BF16)) for d, h in chains}
    r2 = {}
    for d, h in chains:
        v_new = dirs[d][0][h] - r1[(d, h)][:c]
        r2[(d, h)] = _dot(dirs[d][2][h], v_new.astype(BF16))
    for d, h in chains:
        gt = dirs[d][3][...]
        dirs[d][4][0, :, h * LANES:(h + 1) * LANES] = r1[(d, h)][c:] + r2[(d, h)][:c]
        state[d, h] = s_prev[(d, h)] * gt[:, d * heads + h:d * heads + h + 1] + r2[(d, h)][c:]


def _delta(q, k, v, gates, n_lat, heads):
    b, s, vw = q.shape
    c = DN_CHUNK
    n_chunks, n_lat_c = s // c, n_lat // c
    n_ctx_c = n_chunks - n_lat_c
    cpb = _pick(n_chunks, (DN_CHUNKS_PER_STEP, 1))
    seq = pl.BlockSpec((1, cpb * c, vw), lambda b_, i: (b_, i, 0))

    def loc(rows, cols):
        return pl.BlockSpec((2, None, cpb, heads, rows, cols), lambda b_, i: (0, b_, i, 0, 0, 0))

    u, wq, ak, gt = pl.pallas_call(
        functools.partial(_delta_local_kernel, heads=heads, pack=math.gcd(heads, DN_PACK), cpb=cpb),
        grid=(b, n_chunks // cpb),
        in_specs=[seq, seq, seq, pl.BlockSpec((1, cpb * c, LANES), lambda b_, i: (b_, i, 0))],
        out_specs=[loc(c, DN_DV), loc(2 * c, DN_DV), loc(c + DN_DK, c),
                   pl.BlockSpec((2, None, cpb, 1, LANES), lambda b_, i: (0, b_, i, 0, 0))],
        out_shape=[jax.ShapeDtypeStruct((2, b, n_chunks, heads, c, DN_DV), F32),
                   jax.ShapeDtypeStruct((2, b, n_chunks, heads, 2 * c, DN_DV), BF16),
                   jax.ShapeDtypeStruct((2, b, n_chunks, heads, c + DN_DK, c), BF16),
                   jax.ShapeDtypeStruct((2, b, n_chunks, 1, LANES), F32)],
        compiler_params=_params("arbitrary", "arbitrary"), name="delta_local",
    )(q, k, v, gates)

    def fwd(st):
        return jnp.where(st < n_ctx_c, n_lat_c + st, st - n_ctx_c)

    def bwd(st):
        return jnp.where(st < n_ctx_c, n_chunks - 1 - st, n_lat_c - 1 - (st - n_ctx_c))

    def chunk(d, order, rows, cols):
        return pl.BlockSpec((None, None, None, heads, rows, cols), lambda b_, st: (d, b_, order(st), 0, 0, 0))

    def gspec(d, order):
        return pl.BlockSpec((None, None, None, 1, LANES), lambda b_, st: (d, b_, order(st), 0, 0))

    def ospec(order):
        return pl.BlockSpec((1, c, vw), lambda b_, st: (b_, order(st), 0))

    ins, args = [], []
    for d, order in ((0, fwd), (1, bwd)):
        ins += [chunk(d, order, c, DN_DV), chunk(d, order, 2 * c, DN_DV), chunk(d, order, c + DN_DK, c),
                gspec(d, order)]
        args += [u, wq, ak, gt]
    out = jax.ShapeDtypeStruct((b, s, vw), F32)
    return pl.pallas_call(
        functools.partial(_delta_scan_kernel, heads=heads), grid=(b, n_chunks),
        in_specs=ins, out_specs=[ospec(fwd), ospec(bwd)], out_shape=[out, out],
        scratch_shapes=[pltpu.VMEM((2, heads, DN_DK, DN_DV), F32)],
        compiler_params=_params("arbitrary", "arbitrary"), name="delta_scan",
    )(*args)


def _lru_kernel(uf_ref, ub_ref, wri_ref, bri_ref, spl_ref, hf_ref, hb_ref, carry, *, r, n_blk):
    @pl.when(pl.program_id(1) == 0)
    def _():
        carry[...] = jnp.zeros_like(carry)

    row = lax.broadcasted_iota(jnp.int32, (r, 1), 0)
    for d, (u_ref, h_ref) in enumerate(((uf_ref, hf_ref), (ub_ref, hb_ref))):
        for n in range(n_blk):
            sl = slice(n * LANES, (n + 1) * LANES)
            u = u_ref[0, :, sl]
            ri = _dot(u.astype(BF16), wri_ref[d, n]) + bri_ref[d, n]
            rg = _sigmoid(ri[:, :LANES])
            ig = _sigmoid(ri[:, LANES:])
            log_a = -spl_ref[d, n] * rg
            a = jnp.exp(log_a)
            bt = jnp.sqrt(-_expm1_nonpos(2.0 * log_a)) * (ig * u)
            sh = 1
            while sh < r:
                if d == 0:
                    keep = row >= sh
                    a_s, b_s = pltpu.roll(a, sh, 0), pltpu.roll(bt, sh, 0)
                else:
                    keep = row < r - sh
                    a_s, b_s = pltpu.roll(a, r - sh, 0), pltpu.roll(bt, r - sh, 0)
                bt = a * jnp.where(keep, b_s, 0.0) + bt
                a = a * jnp.where(keep, a_s, 1.0)
                sh *= 2
            h = bt + a * carry[d, :, sl]
            h_ref[0, :, sl] = h
            carry[d, :, sl] = h[r - 1:r] if d == 0 else h[0:1]


def _lru(u, w_ri, b_ri, spl, r, n_lat):
    b, s, width = u.shape
    n_blk = width // LANES
    nb, nlb = s // r, n_lat // r
    ncb = nb - nlb

    def fwd(st):
        return jnp.where(st < ncb, nlb + st, st - ncb)

    def bwd(st):
        return jnp.where(st < ncb, nb - 1 - st, nlb - 1 - (st - ncb))

    def seq(order):
        return pl.BlockSpec((1, r, width), lambda b_, st: (b_, order(st), 0))

    def par(shape):
        return pl.BlockSpec(shape, lambda b_, st: (0,) * len(shape))

    out = jax.ShapeDtypeStruct((b, s, width), F32)
    return pl.pallas_call(
        functools.partial(_lru_kernel, r=r, n_blk=n_blk), grid=(b, nb),
        in_specs=[seq(fwd), seq(bwd), par(w_ri.shape), par(b_ri.shape), par(spl.shape)],
        out_specs=[seq(fwd), seq(bwd)], out_shape=[out, out],
        scratch_shapes=[pltpu.VMEM((2, 1, width), F32)],
        compiler_params=_params("arbitrary", "arbitrary"), name="lru",
    )(u, u, w_ri, b_ri, spl)


def _gelu_tanh(x):
    return 0.5 * x * (1.0 + jnp.tanh(math.sqrt(2.0 / math.pi) * (x + 0.044715 * (x * x * x))))


def _even_out_kernel(of_ref, ob_ref, z_ref, hf_ref, hb_ref, y_ref, on_ref, mix_ref, *, heads, n_blk):
    for h in range(heads):
        sl = slice(h * LANES, (h + 1) * LANES)
        o = of_ref[0, :, sl] + ob_ref[0, :, sl]
        o = o * lax.rsqrt(jnp.mean(o * o, axis=-1, keepdims=True) + NORM_EPS) * on_ref[...]
        mix_ref[0, :, sl] = (o * _silu(z_ref[0, :, sl])).astype(BF16)
    for n in range(n_blk):
        sl = slice(n * LANES, (n + 1) * LANES)
        osl = slice((heads + n) * LANES, (heads + n + 1) * LANES)
        mix_ref[0, :, osl] = ((hf_ref[0, :, sl] + hb_ref[0, :, sl]) * _gelu_tanh(y_ref[0, :, sl])).astype(BF16)


def _even_out(o_f, o_b, p, h_f, h_b, o_norm, r, heads):
    b, s, vw = o_f.shape
    width = h_f.shape[2]

    def cur(wd, cb):
        return pl.BlockSpec((1, r, wd), lambda b_, i: (b_, i, cb))

    zcb = (3 * vw) // vw
    ycb = (4 * vw + width) // width
    return pl.pallas_call(
        functools.partial(_even_out_kernel, heads=heads, n_blk=width // LANES), grid=(b, s // r),
        in_specs=[cur(vw, 0), cur(vw, 0), cur(vw, zcb), cur(width, 0), cur(width, 0), cur(width, ycb),
                  pl.BlockSpec((1, DN_DV), lambda b_, i: (0, 0))],
        out_specs=cur(vw + width, 0), out_shape=jax.ShapeDtypeStruct((b, s, vw + width), BF16),
        compiler_params=_params("arbitrary", "arbitrary"), name="even_out",
    )(o_f, o_b, p, h_f, h_b, p, o_norm.reshape(1, DN_DV))


def _rope_pair(y, tab):
    y = y * tab
    return y + pltpu.roll(y, MLA_ROPE, 1)


def _q_proj_kernel(a_ref, w_ref, tab_ref, q_ref, *, scale):
    acc = _dot(a_ref[0], w_ref[0])
    q_ref[0, 0, :, 0:MLA_NOPE] = (acc[:, :MLA_NOPE] * scale).astype(BF16)
    qr = _rope_pair(acc[:, MLA_NOPE:], tab_ref[...])
    q_ref[0, 0, :, MLA_NOPE:MLA_QD] = (qr[:, :MLA_ROPE] * scale).astype(BF16)


def _q_proj(h, w_q, tab, heads):
    b, s, d = h.shape
    tm = _pick(s, (768, 512, 640, 256, 128))
    return pl.pallas_call(
        functools.partial(_q_proj_kernel, scale=math.log2(math.e) * MLA_QD ** -0.5), grid=(b, s // tm, heads),
        in_specs=[pl.BlockSpec((1, tm, d), lambda b_, i, hd: (b_, i, 0)),
                  pl.BlockSpec((1, d, 2 * LANES), lambda b_, i, hd: (hd, 0, 0)),
                  pl.BlockSpec((tm, LANES), lambda b_, i, hd: (i, 0))],
        out_specs=pl.BlockSpec((1, 1, tm, MLA_QD), lambda b_, i, hd: (b_, hd, i, 0)),
        out_shape=jax.ShapeDtypeStruct((b, heads, s, MLA_QD), BF16),
        compiler_params=_params("arbitrary", "arbitrary", "arbitrary"), name="q_proj",
    )(h, w_q, tab)


def _ckv_proj_kernel(a_ref, w_ref, tab_ref, g_ref, ckv_ref, kr_ref, *, rank):
    acc = _dot(a_ref[0], w_ref[...])
    ckv = acc[:, :rank]
    ckv = ckv * lax.rsqrt(jnp.mean(ckv * ckv, axis=-1, keepdims=True) + NORM_EPS) * g_ref[...]
    ckv_ref[0] = ckv.astype(BF16)
    kr_ref[0] = _rope_pair(acc[:, rank:], tab_ref[...]).astype(BF16)


def _ckv_proj(h, w_c, tab, kv_norm):
    b, s, d = h.shape
    rank = kv_norm.shape[0]
    tm = _pick(s, (768, 512, 640, 256, 128))
    return pl.pallas_call(
        functools.partial(_ckv_proj_kernel, rank=rank), grid=(b, s // tm),
        in_specs=[pl.BlockSpec((1, tm, d), lambda b_, i: (b_, i, 0)),
                  pl.BlockSpec((d, rank + LANES), lambda b_, i: (0, 0)),
                  pl.BlockSpec((tm, LANES), lambda b_, i: (i, 0)),
                  pl.BlockSpec((1, rank), lambda b_, i: (0, 0))],
        out_specs=[pl.BlockSpec((1, tm, rank), lambda b_, i: (b_, i, 0)),
                   pl.BlockSpec((1, tm, LANES), lambda b_, i: (b_, i, 0))],
        out_shape=[jax.ShapeDtypeStruct((b, s, rank), BF16), jax.ShapeDtypeStruct((b, s, LANES), BF16)],
        compiler_params=_params("arbitrary", "arbitrary"), name="ckv_proj",
    )(h, w_c, tab, kv_norm.reshape(1, rank))


def _kv_up_kernel(a_ref, w_ref, kr_ref, k_ref, v_ref):
    acc = _dot(a_ref[0], w_ref[0])
    k_ref[0, 0, :, 0:MLA_NOPE] = acc[:, :MLA_NOPE].astype(BF16)
    k_ref[0, 0, :, MLA_NOPE:MLA_QD] = kr_ref[0, :, 0:MLA_ROPE]
    v_ref[0, 0] = acc[:, MLA_NOPE:].astype(BF16)


def _kv_up(ckv, w_ukv, kr, heads):
    b, s, rank = ckv.shape
    tm = _pick(s, (768, 512, 640, 256, 128))
    return pl.pallas_call(
        _kv_up_kernel, grid=(b, s // tm, heads),
        in_specs=[pl.BlockSpec((1, tm, rank), lambda b_, i, hd: (b_, i, 0)),
                  pl.BlockSpec((1, rank, MLA_NOPE + MLA_V), lambda b_, i, hd: (hd, 0, 0)),
                  pl.BlockSpec((1, tm, LANES), lambda b_, i, hd: (b_, i, 0))],
        out_specs=[pl.BlockSpec((1, 1, tm, MLA_QD), lambda b_, i, hd: (b_, hd, i, 0)),
                   pl.BlockSpec((1, 1, tm, MLA_V), lambda b_, i, hd: (b_, hd, i, 0))],
        out_shape=[jax.ShapeDtypeStruct((b, heads, s, MLA_QD), BF16),
                   jax.ShapeDtypeStruct((b, heads, s, MLA_V), BF16)],
        compiler_params=_params("arbitrary", "arbitrary", "arbitrary"), name="kv_up",
    )(ckv, w_ukv, kr)


def _flash_kernel(q_ref, k_ref, v_ref, o_ref, m_ref, l_ref, acc_ref, *, sub):
    j = pl.program_id(3)

    @pl.when(j == 0)
    def _():
        m_ref[...] = jnp.full_like(m_ref, -jnp.inf)
        l_ref[...] = jnp.zeros_like(l_ref)
        acc_ref[...] = jnp.zeros_like(acc_ref)

    k = k_ref[0, 0]
    v = v_ref[0, 0]
    for c in range(q_ref.shape[2] // sub):
        rows = slice(c * sub, (c + 1) * sub)
        s = _dot_nt(q_ref[0, 0, rows, :], k)
        m_prev = m_ref[rows]
        m_new = jnp.maximum(m_prev, jnp.max(s, axis=-1, keepdims=True))
        p = jnp.exp2(s - m_new)
        alpha = jnp.exp2(m_prev - m_new)
        l_ref[rows] = alpha * l_ref[rows] + jnp.sum(p, axis=-1, keepdims=True)
        acc_ref[rows] = alpha * acc_ref[rows] + _dot(p.astype(BF16), v)
        m_ref[rows] = m_new

    @pl.when(j == pl.num_programs(3) - 1)
    def _():
        o_ref[0] = (acc_ref[...] / l_ref[...]).astype(o_ref.dtype)


def _attn_full_kernel(q_ref, k_ref, v_ref, o_ref, *, sub):
    k = k_ref[0, 0]
    v = v_ref[0, 0]
    for c in range(q_ref.shape[2] // sub):
        rows = slice(c * sub, (c + 1) * sub)
        s = _dot_nt(q_ref[0, 0, rows, :], k)
        p = jnp.exp2(s - jnp.max(s, axis=-1, keepdims=True))
        l = jnp.sum(p, axis=-1, keepdims=True)
        o_ref[0, rows, :] = (_dot(p.astype(BF16), v) / l).astype(o_ref.dtype)


def _flash(q, k, v, q_rows, kv_rows):
    b, heads, _, dq = q.shape
    dv = v.shape[3]
    q0, nq = q_rows
    k0, nk = kv_rows
    tq = _pick(math.gcd(nq, q0) if q0 else nq, (1024, 512, 256, 128))
    tk = _pick(math.gcd(nk, k0) if k0 else nk, (nk, 768, 512, 256, 128))
    qo, ko = q0 // tq, k0 // tk
    if nk == tk:
        return pl.pallas_call(
            functools.partial(_attn_full_kernel, sub=min(tq, FLASH_SUB)), grid=(b, heads, nq // tq),
            in_specs=[pl.BlockSpec((1, 1, tq, dq), lambda b_, h, i: (b_, h, qo + i, 0)),
                      pl.BlockSpec((1, 1, tk, dq), lambda b_, h, i: (b_, h, ko, 0)),
                      pl.BlockSpec((1, 1, tk, dv), lambda b_, h, i: (b_, h, ko, 0))],
            out_specs=pl.BlockSpec((1, tq, dv), lambda b_, h, i: (b_, i, h)),
            out_shape=jax.ShapeDtypeStruct((b, nq, heads * dv), BF16),
            compiler_params=_params("arbitrary", "arbitrary", "arbitrary"), name="attn_full",
        )(q, k, v)
    return pl.pallas_call(
        functools.partial(_flash_kernel, sub=min(tq, FLASH_SUB)), grid=(b, heads, nq // tq, nk // tk),
        in_specs=[pl.BlockSpec((1, 1, tq, dq), lambda b_, h, i, j: (b_, h, qo + i, 0)),
                  pl.BlockSpec((1, 1, tk, dq), lambda b_, h, i, j: (b_, h, ko + j, 0)),
                  pl.BlockSpec((1, 1, tk, dv), lambda b_, h, i, j: (b_, h, ko + j, 0))],
        out_specs=pl.BlockSpec((1, tq, dv), lambda b_, h, i, j: (b_, i, h)),
        out_shape=jax.ShapeDtypeStruct((b, nq, heads * dv), BF16),
        scratch_shapes=[pltpu.VMEM((tq, 1), F32), pltpu.VMEM((tq, 1), F32), pltpu.VMEM((tq, dv), F32)],
        compiler_params=_params("arbitrary", "arbitrary", "arbitrary", "arbitrary"), name="flash",
    )(q, k, v)


def _experts_kernel(be_ref, nu_ref, x_ref, w1_ref, w3_ref, w2_ref, y_ref):
    @pl.when(pl.program_id(0) < nu_ref[0])
    def _():
        x = x_ref[...].astype(BF16)
        h1 = _dot(x, w1_ref[0].astype(BF16))
        h3 = _dot(x, w3_ref[0].astype(BF16))
        hh = (_silu(h1) * h3).astype(BF16)
        y_ref[...] = _dot(hh, w2_ref[0].astype(BF16))


def _experts(xg, blk_e, n_used, w1, w3, w2, layer, tm):
    rows, d = xg.shape
    de = w1.shape[3]
    grid_spec = pltpu.PrefetchScalarGridSpec(
        num_scalar_prefetch=2, grid=(rows // tm,),
        in_specs=[pl.BlockSpec((tm, d), lambda i, be, nu: (i, 0)),
                  pl.BlockSpec((None, 1, d, de), lambda i, be, nu: (layer, be[i], 0, 0)),
                  pl.BlockSpec((None, 1, d, de), lambda i, be, nu: (layer, be[i], 0, 0)),
                  pl.BlockSpec((None, 1, de, d), lambda i, be, nu: (layer, be[i], 0, 0))],
        out_specs=pl.BlockSpec((tm, d), lambda i, be, nu: (i, 0)))
    return pl.pallas_call(
        _experts_kernel, grid_spec=grid_spec, out_shape=jax.ShapeDtypeStruct((rows, d), F32),
        compiler_params=_params("arbitrary"), name="experts",
    )(blk_e, n_used, xg, w1, w3, w2)


def _moe(h, logits, n_groups, n_experts, w1, w3, w2, layer, tm):
    n_tok, d = h.shape
    epg = n_experts // n_groups
    lg = logits[:, :n_groups]
    grp = jnp.argmax(lg, axis=-1)
    p_grp = jnp.take_along_axis(jax.nn.softmax(lg, -1), grp[:, None], -1)
    le = logits[:, n_groups:n_groups + n_experts].reshape(n_tok, n_groups, epg)
    le = jnp.take_along_axis(le, grp[:, None, None], axis=1)[:, 0]
    top_v, top_i = lax.top_k(le, TOP_K)
    gate = p_grp * jax.nn.softmax(top_v, -1)
    expert = (grp[:, None] * epg + top_i).reshape(-1).astype(jnp.int32)
    onehot = (expert[:, None] == jnp.arange(n_experts, dtype=jnp.int32)[None, :]).astype(jnp.int32)
    rank = jnp.sum((jnp.cumsum(onehot, axis=0) - onehot) * onehot, axis=1)
    counts = jnp.sum(onehot, axis=0)
    padded = (counts + tm - 1) // tm * tm
    pend = jnp.cumsum(padded)
    dest = (pend - padded)[expert] + rank
    n_blk = (n_tok * TOP_K + tm - 1) // tm + n_experts
    blk_e = jnp.minimum(jnp.sum(pend[None, :] <= (jnp.arange(n_blk) * tm)[:, None], axis=1), n_experts - 1)
    n_used = (pend[-1] // tm).reshape(1)
    src = (jnp.arange(n_blk * tm, dtype=jnp.int32) % n_tok).at[dest].set(
        jnp.arange(n_tok * TOP_K, dtype=jnp.int32) // TOP_K)
    y = _experts(h[src], blk_e.astype(jnp.int32), n_used.astype(jnp.int32), w1, w3, w2, layer, tm)
    dest = dest.reshape(n_tok, TOP_K)
    return (y[dest[:, 0]], y[dest[:, 1]]), (gate[:, 0:1], gate[:, 1:2])


def _rope_table(n_lat, n_ctx):
    rows = n_lat // GRID_W
    row = jnp.broadcast_to(jnp.arange(rows)[:, None], (rows, GRID_W)).reshape(-1)
    col = jnp.broadcast_to(jnp.arange(GRID_W)[None, :], (rows, GRID_W)).reshape(-1)
    pos = jnp.stack([row, col], -1).astype(F32)
    inv = ROPE_BASE ** (-jnp.arange(ROPE_FREQS, dtype=F32) / ROPE_FREQS)
    ang = pos[:, :, None] * inv
    ang = jnp.broadcast_to(ang[:, :, None, :], (n_lat, 2, 2, ROPE_FREQS)).reshape(n_lat, MLA_ROPE)
    lat = jnp.concatenate([jnp.cos(ang), jnp.sin(ang)], axis=-1)
    ctx = jnp.concatenate([jnp.ones((n_ctx, MLA_ROPE), F32), jnp.zeros((n_ctx, MLA_ROPE), F32)], axis=-1)
    return jnp.concatenate([lat, ctx], axis=0)


def _rot_cols(w):
    wr = w.reshape(w.shape[:-1] + (2, 2, ROPE_FREQS))
    return jnp.stack([-wr[..., 1, :], wr[..., 0, :]], axis=-2).reshape(w.shape)


def kernel(x, c, ctx, c_ctx, ada_w, ada_b, ln_mix_g, ln_mix_b, ln_ffn_g, ln_ffn_b, ev_w_in, ev_conv_qkv, ev_a_log,
           ev_dt_bias, ev_o_norm, ev_conv_x_w, ev_conv_x_b, ev_w_r, ev_b_r, ev_w_i, ev_b_i, ev_lam, ev_w_out,
           od_w_in, od_kv_norm, od_w_ukv, od_w_out, moe_w_grp, moe_b_grp, moe_w_exp, moe_b_exp, moe_w1, moe_w3,
           moe_w2):
    bsz, n_lat, d = x.shape
    n_ctx = ctx.shape[1]
    s = n_lat + n_ctx
    depth = ada_w.shape[0]
    alpha = (2.0 * depth) ** 0.25
    r = min(256, n_ctx)
    assert n_lat % r == 0 and n_ctx % r == 0 and n_lat % GRID_W == 0 and bsz + 1 <= 8
    dn_heads = ev_a_log.shape[-1]
    vw = dn_heads * DN_DV
    width = ev_lam.shape[-1]
    lru_blocks = ev_w_r.shape[2]
    assert width // lru_blocks == LANES
    rank = od_kv_norm.shape[-1]
    mla_heads = od_w_ukv.shape[-1] // (MLA_NOPE + MLA_V)
    n_groups, n_experts = moe_w_grp.shape[-1], moe_w_exp.shape[-1]
    moe_tm = 256 if (bsz * s * TOP_K) // n_experts >= 512 else 128

    xs = jnp.concatenate([x, ctx], axis=1)
    c_all = jnp.zeros((8, d), F32).at[:bsz].set(c).at[bsz].set(c_ctx)
    mods = _adaln(c_all, ada_w, ada_b).reshape(depth, 8, 6, d)

    def seg_tab(layer, k):
        lat = mods[layer, :bsz, k]
        ctx_v = jnp.broadcast_to(mods[layer, bsz, k][None], (bsz, d))
        return jnp.stack([lat, ctx_v], axis=1)[:, :, None, :]

    rope_tab = _rope_table(n_lat, n_ctx)
    hmod = _modulate(xs, seg_tab(0, 0), seg_tab(0, 1), r, n_lat)
    for layer in range(depth):
        i = layer // 2
        last = layer == depth - 1
        if layer % 2 == 0:
            qkvw = 3 * vw
            w = ev_w_in[i]
            o_z, o_a, o_b, o_x, o_y = qkvw, qkvw + vw, qkvw + vw + 2 * dn_heads, qkvw + vw + 4 * dn_heads, \
                qkvw + vw + 4 * dn_heads + width
            w_perm = jnp.concatenate([w[:, :o_a], w[:, o_x:], w[:, o_a:o_x],
                                      jnp.zeros((d, LANES - 4 * dn_heads), F32)], axis=1).astype(BF16)
            p = _mm(hmod.reshape(bsz * s, d), w_perm).reshape(bsz, s, -1)
            q, k, v, u, gb = _even_prep(p, ev_conv_qkv[i], ev_conv_x_w[i], ev_conv_x_b[i], -jnp.exp(ev_a_log[i]),
                                        ev_dt_bias[i], r, n_lat, dn_heads, width)
            o_f, o_b = _delta(q, k, v, gb, n_lat, dn_heads)
            w_ri = jnp.concatenate([ev_w_r[i], ev_w_i[i]], axis=-1).astype(BF16)
            b_ri = jnp.concatenate([ev_b_r[i].reshape(2, lru_blocks, 1, LANES),
                                    ev_b_i[i].reshape(2, lru_blocks, 1, LANES)], axis=-1)
            spl = (LRU_C * jax.nn.softplus(-ev_lam[i])).reshape(2, lru_blocks, 1, LANES)
            h_f, h_b = _lru(u, w_ri, b_ri, spl, r, n_lat)
            mix = _even_out(o_f, o_b, p, h_f, h_b, ev_o_norm[i], r, dn_heads)
            w_out = ev_w_out[i]
        else:
            w = od_w_in[i]
            nq = mla_heads * MLA_QD
            wq = w[:, :nq].reshape(d, mla_heads, MLA_QD)
            wq = jnp.concatenate([wq, _rot_cols(wq[..., MLA_NOPE:])], axis=-1).transpose(1, 0, 2).astype(BF16)
            w_kr = w[:, nq + rank:]
            w_c = jnp.concatenate([w[:, nq:nq + rank], w_kr, _rot_cols(w_kr)], axis=-1).astype(BF16)
            w_u = od_w_ukv[i].reshape(rank, mla_heads, MLA_NOPE + MLA_V).transpose(1, 0, 2).astype(BF16)
            qh = _q_proj(hmod, wq, rope_tab, mla_heads)
            ckv, kr = _ckv_proj(hmod, w_c, rope_tab, od_kv_norm[i])
            kh, vh = _kv_up(ckv, w_u, kr, mla_heads)
            mix = jnp.concatenate([_flash(qh, kh, vh, (0, n_lat), (0, s)),
                                   _flash(qh, kh, vh, (n_lat, n_ctx), (n_lat, n_ctx))], axis=1)
            w_out = od_w_out[i]
        y = _mm(mix.reshape(bsz * s, -1), w_out.astype(BF16)).reshape(bsz, s, d)
        nr = LANES * ((n_groups + n_experts + LANES - 1) // LANES)
        w_rt = jnp.pad(jnp.concatenate([moe_w_grp[layer], moe_w_exp[layer]], axis=1),
                       ((0, 0), (0, nr - n_groups - n_experts)))
        b_rt = jnp.pad(jnp.concatenate([moe_b_grp[layer], moe_b_exp[layer]]), (0, nr - n_groups - n_experts))
        xs, hf, logits = _postnorm(xs, [y], seg_tab(layer, 2), ln_mix_g[layer], ln_mix_b[layer], r, n_lat, alpha,
                                   shift=seg_tab(layer, 3), scale=seg_tab(layer, 4), router=(w_rt, b_rt[None]),
                                   h_dtype=F32)
        ys, gts = _moe(hf.reshape(bsz * s, d), logits.reshape(bsz * s, nr), n_groups, n_experts,
                       moe_w1, moe_w3, moe_w2, layer, moe_tm)
        ys = [t.reshape(bsz, s, d) for t in ys]
        gts = [t.reshape(bsz, s, 1) for t in gts]
        if last:
            (xs,) = _postnorm(xs, ys, seg_tab(layer, 5), ln_ffn_g[layer], ln_ffn_b[layer], r, n_lat, alpha,
                              rows=n_lat, row_gates=gts)
        else:
            xs, hmod = _postnorm(xs, ys, seg_tab(layer, 5), ln_ffn_g[layer], ln_ffn_b[layer], r, n_lat, alpha,
                                 shift=seg_tab(layer + 1, 0), scale=seg_tab(layer + 1, 1), row_gates=gts)
    return xs
```

```python
import functools
import math

import jax
import jax.numpy as jnp
from jax import lax
from jax.experimental import pallas as pl
from jax.experimental.pallas import tpu as pltpu

F32 = jnp.float32
BF16 = jnp.bfloat16
HIGHEST = lax.Precision.HIGHEST

LANES = 128
DN_DK = 128
DN_DV = 128
DN_CHUNK = 64
MLA_NOPE = 128
MLA_ROPE = 64
MLA_V = 128
MLA_QD = MLA_NOPE + MLA_ROPE
GRID_W = 64
ROPE_FREQS = MLA_ROPE // 4
ROPE_BASE = 10000.0
LRU_C = 8.0
LN_EPS = 1e-5
NORM_EPS = 1e-6
TOP_K = 2
FLASH_SUB = 256
VMEM_LIMIT = 48 * 1024 * 1024


def _pick(n, cands):
    for c in cands:
        if n % c == 0:
            return c
    raise ValueError(f"no tile for {n} in {cands}")


def _params(*sem):
    return pltpu.CompilerParams(dimension_semantics=sem, vmem_limit_bytes=VMEM_LIMIT)


def _dot(a, b, precision=None):
    return jnp.dot(a, b, preferred_element_type=F32, precision=precision)


def _dot_nt(a, b, precision=None):
    return lax.dot_general(a, b, (((1,), (1,)), ((), ())), preferred_element_type=F32, precision=precision)


def _sigmoid(x):
    return 1.0 / (1.0 + jnp.exp(-x))


def _silu(x):
    return x * _sigmoid(x)


def _softplus(x):
    return jnp.maximum(x, 0.0) + jnp.log1p(jnp.exp(-jnp.abs(x)))


def _expm1_nonpos(x):
    u = jnp.exp(x)
    safe = (x > -0.5) & (u < 1.0)
    stable = (u - 1.0) * x / jnp.log(jnp.where(safe, u, 0.5))
    return jnp.where(safe, stable, jnp.where(u < 1.0, u - 1.0, x))


def _adaln_kernel(c_ref, w_ref, b_ref, o_ref):
    a = _silu(c_ref[...])
    o_ref[0] = _dot(a.astype(BF16), w_ref[0].astype(BF16)) + b_ref[0]


def _adaln(c_all, ada_w, ada_b):
    n_layer, d, n6 = ada_w.shape
    tn = _pick(n6, (1024, 512, 256, 128))
    return pl.pallas_call(
        _adaln_kernel, grid=(n_layer, n6 // tn),
        in_specs=[pl.BlockSpec((8, d), lambda l, j: (0, 0)),
                  pl.BlockSpec((1, d, tn), lambda l, j: (l, 0, j)),
                  pl.BlockSpec((1, 1, tn), lambda l, j: (l, 0, j))],
        out_specs=pl.BlockSpec((1, 8, tn), lambda l, j: (l, 0, j)),
        out_shape=jax.ShapeDtypeStruct((n_layer, 8, n6), F32),
        compiler_params=_params("arbitrary", "arbitrary"), name="adaln",
    )(c_all, ada_w, ada_b.reshape(n_layer, 1, n6))


def _seg_spec(d, n_lat_blocks):
    return pl.BlockSpec((1, 1, 1, d), lambda b, i: (b, jnp.where(i >= n_lat_blocks, 1, 0), 0, 0))


def _modulate_kernel(x_ref, sh_ref, sc_ref, h_ref):
    h_ref[0] = (x_ref[0] * (1.0 + sc_ref[0, 0]) + sh_ref[0, 0]).astype(BF16)


def _modulate(x, shift, scale, r, n_lat):
    b, s, d = x.shape
    row = pl.BlockSpec((1, r, d), lambda b_, i: (b_, i, 0))
    return pl.pallas_call(
        _modulate_kernel, grid=(b, s // r),
        in_specs=[row, _seg_spec(d, n_lat // r), _seg_spec(d, n_lat // r)],
        out_specs=row, out_shape=jax.ShapeDtypeStruct((b, s, d), BF16),
        compiler_params=_params("arbitrary", "arbitrary"), name="modulate",
    )(x, shift, scale)


def _postnorm_kernel(*refs, alpha, n_y, row_gated, with_h, with_router):
    x_ref = refs[0]
    pos = 1
    y = None
    for _ in range(n_y):
        term = refs[pos][0].astype(F32)
        pos += 1
        if row_gated:
            term = term * refs[pos][0]
            pos += 1
        y = term if y is None else y + term
    gate_ref, g_ref, b_ref = refs[pos:pos + 3]
    pos += 3
    if with_h:
        sh_ref, sc_ref = refs[pos:pos + 2]
        pos += 2
    if with_router:
        wr_ref, br_ref = refs[pos:pos + 2]
        pos += 2
    xo_ref = refs[pos]
    v = alpha * x_ref[0] + gate_ref[0, 0] * y
    mu = jnp.mean(v, axis=-1, keepdims=True)
    vc = v - mu
    var = jnp.mean(vc * vc, axis=-1, keepdims=True)
    xn = vc * lax.rsqrt(var + LN_EPS) * g_ref[...] + b_ref[...]
    xo_ref[0] = xn
    if with_h:
        h = xn * (1.0 + sc_ref[0, 0]) + sh_ref[0, 0]
        refs[pos + 1][0] = h.astype(BF16)
        if with_router:
            refs[pos + 2][0] = _dot(h, wr_ref[...], HIGHEST) + br_ref[...]


def _postnorm(x, ys, gate, ln_g, ln_b, r, n_lat, alpha, shift=None, scale=None, router=None, rows=None,
              row_gates=None):
    b, s, d = x.shape
    rows = s if rows is None else rows
    nlb = n_lat // r
    row = pl.BlockSpec((1, r, d), lambda b_, i: (b_, i, 0))
    col1 = pl.BlockSpec((1, r, 1), lambda b_, i: (b_, i, 0))
    vec = pl.BlockSpec((1, d), lambda b_, i: (0, 0))
    args, in_specs = [x], [row]
    for j, y in enumerate(ys):
        args.append(y)
        in_specs.append(row)
        if row_gates is not None:
            args.append(row_gates[j])
            in_specs.append(col1)
    args += [gate, ln_g.reshape(1, d), ln_b.reshape(1, d)]
    in_specs += [_seg_spec(d, nlb), vec, vec]
    out_shape = [jax.ShapeDtypeStruct((b, rows, d), F32)]
    out_specs = [row]
    with_h = shift is not None
    if with_h:
        args += [shift, scale]
        in_specs += [_seg_spec(d, nlb), _seg_spec(d, nlb)]
        out_shape.append(jax.ShapeDtypeStruct((b, rows, d), BF16))
        out_specs.append(row)
    if router is not None:
        w_r, b_r = router
        nr = w_r.shape[1]
        args += [w_r, b_r]
        in_specs += [pl.BlockSpec((d, nr), lambda b_, i: (0, 0)), pl.BlockSpec((1, nr), lambda b_, i: (0, 0))]
        out_shape.append(jax.ShapeDtypeStruct((b, rows, nr), F32))
        out_specs.append(pl.BlockSpec((1, r, nr), lambda b_, i: (b_, i, 0)))
    return pl.pallas_call(
        functools.partial(_postnorm_kernel, alpha=alpha, n_y=len(ys), row_gated=row_gates is not None,
                          with_h=with_h, with_router=router is not None),
        grid=(b, rows // r), in_specs=in_specs, out_specs=out_specs, out_shape=out_shape,
        compiler_params=_params("arbitrary", "arbitrary"), name="postnorm",
    )(*args)


def _mm_kernel(a_ref, b_ref, o_ref):
    o_ref[...] = _dot(a_ref[...].astype(BF16), b_ref[...].astype(BF16)).astype(o_ref.dtype)


def _mm(a, b, out_dtype=F32):
    m, k = a.shape
    n = b.shape[1]
    tm = _pick(m, (512, 256, 128, 64, 8))
    tn = _pick(n, (1024, 896, 768, 640, 512, 384, 256, 128))
    return pl.pallas_call(
        _mm_kernel, grid=(m // tm, n // tn),
        in_specs=[pl.BlockSpec((tm, k), lambda i, j: (i, 0)), pl.BlockSpec((k, tn), lambda i, j: (0, j))],
        out_specs=pl.BlockSpec((tm, tn), lambda i, j: (i, j)),
        out_shape=jax.ShapeDtypeStruct((m, n), out_dtype),
        compiler_params=_params("arbitrary", "arbitrary"), name="mm",
    )(a, b)


def _even_prep_kernel(qkv_ref, qkv_p_ref, qkv_n_ref, xr_ref, xr_p_ref, xr_n_ref, ab_ref,
                      cw_ref, xw_ref, xb_ref, nea_ref, dtb_ref,
                      q_ref, k_ref, v_ref, u_ref, gb_ref, *, n_lat_blocks, n_blocks, heads, r):
    i = pl.program_id(1)
    pv = jnp.where((i != 0) & (i != n_lat_blocks), 1.0, 0.0)
    nv = jnp.where((i != n_lat_blocks - 1) & (i != n_blocks - 1), 1.0, 0.0)
    row = lax.broadcasted_iota(jnp.int32, (r, 1), 0)

    def conv(x, p8, n8, w):
        p8 = p8 * pv
        n8 = n8 * nv
        xm1 = jnp.where(row == 0, p8[7:8], pltpu.roll(x, 1, 0))
        xm2 = jnp.where(row == 0, p8[6:7], jnp.where(row == 1, p8[7:8], pltpu.roll(x, 2, 0)))
        xp1 = jnp.where(row == r - 1, n8[0:1], pltpu.roll(x, r - 1, 0))
        return w[0:1] * xm2 + w[1:2] * xm1 + w[2:3] * x + w[3:4] * xp1

    for j in range(3 * heads):
        sl = slice(j * LANES, (j + 1) * LANES)
        y = _silu(conv(qkv_ref[0, :, sl], qkv_p_ref[0, :, sl], qkv_n_ref[0, :, sl], cw_ref[:, sl]))
        if j < 2 * heads:
            y = y * lax.rsqrt(jnp.sum(y * y, axis=-1, keepdims=True) + NORM_EPS)
        if j < heads:
            q_ref[0, :, sl] = y * (DN_DK ** -0.5)
        elif j < 2 * heads:
            k_ref[0, :, slice((j - heads) * LANES, (j - heads + 1) * LANES)] = y
        else:
            v_ref[0, :, slice((j - 2 * heads) * LANES, (j - 2 * heads + 1) * LANES)] = y
    for j in range(xr_ref.shape[2] // LANES):
        sl = slice(j * LANES, (j + 1) * LANES)
        u_ref[0, :, sl] = conv(xr_ref[0, :, sl], xr_p_ref[0, :, sl], xr_n_ref[0, :, sl], xw_ref[:, sl]) + xb_ref[:, sl]
    ab = ab_ref[0]
    lane = lax.broadcasted_iota(jnp.int32, ab.shape, 1)
    g = nea_ref[...] * _softplus(ab + dtb_ref[...])
    gb_ref[0] = jnp.where(lane < 2 * heads, g, _sigmoid(ab))


def _even_prep(p, conv_qkv, conv_x_w, conv_x_b, neg_exp_a, dt_bias, r, n_lat, heads, width):
    b, s, _ = p.shape
    vw = heads * DN_DV
    qkvw = 3 * vw
    assert width == vw and qkvw % width == 0
    nb, nlb, r8 = s // r, n_lat // r, r // 8
    n8 = s // 8

    def cur(wd, cb):
        return pl.BlockSpec((1, r, wd), lambda b_, i: (b_, i, cb))

    def prev(wd, cb):
        return pl.BlockSpec((1, 8, wd), lambda b_, i: (b_, jnp.maximum(i * r8 - 1, 0), cb))

    def nxt(wd, cb):
        return pl.BlockSpec((1, 8, wd), lambda b_, i: (b_, jnp.minimum((i + 1) * r8, n8 - 1), cb))

    def par(shape):
        return pl.BlockSpec(shape, lambda b_, i: (0, 0))

    xcb = (qkvw + vw) // width
    abcb = (qkvw + vw + 2 * width) // LANES
    pad = LANES - 2 * heads
    nea = jnp.pad(neg_exp_a.reshape(1, 2 * heads), ((0, 0), (0, pad)))
    dtb = jnp.pad(dt_bias.reshape(1, 2 * heads), ((0, 0), (0, pad)))
    outs = [jax.ShapeDtypeStruct((b, s, vw), F32)] * 3 + [jax.ShapeDtypeStruct((b, s, width), F32),
                                                            jax.ShapeDtypeStruct((b, s, LANES), F32)]
    return pl.pallas_call(
        functools.partial(_even_prep_kernel, n_lat_blocks=nlb, n_blocks=nb, heads=heads, r=r),
        grid=(b, nb),
        in_specs=[cur(qkvw, 0), prev(qkvw, 0), nxt(qkvw, 0), cur(width, xcb), prev(width, xcb), nxt(width, xcb),
                  cur(LANES, abcb), par((4, qkvw)), par((4, width)), par((1, width)), par((1, LANES)), par((1, LANES))],
        out_specs=[cur(vw, 0), cur(vw, 0), cur(vw, 0), cur(width, 0), cur(LANES, 0)],
        out_shape=outs, compiler_params=_params("arbitrary", "arbitrary"), name="even_prep",
    )(p, p, p, p, p, p, p, conv_qkv, conv_x_w, conv_x_b.reshape(1, width), nea, dtb)


def _split2(x):
    hi = x.astype(BF16)
    return hi, (x - hi.astype(F32)).astype(BF16)


DN_CHUNKS_PER_STEP = 2
DN_PACK = 4


def _block_diag(x, row_blk, lane_blk, pack):
    tiled = jnp.concatenate([x] * pack, axis=0)
    return jnp.where(row_blk == lane_blk, tiled, 0.0).astype(BF16)


def _dot_hl(a, b_bd):
    n = a[0].shape[0]
    r = _dot(jnp.concatenate([a[0], a[1]], axis=0), b_bd)
    return r[:n] + r[n:]


def _delta_local_kernel(q_ref, k_ref, v_ref, g_ref, u_ref, wq_ref, ak_ref, gt_ref, *, heads, pack, cpb):
    c = DN_CHUNK
    wa = pack * c
    wk = pack * LANES
    ri = lax.broadcasted_iota(jnp.int32, (c, wa), 0)
    la = lax.broadcasted_iota(jnp.int32, (c, wa), 1)
    ci = la & (c - 1)
    blk_a = la >> 6
    blk_k = lax.broadcasted_iota(jnp.int32, (c, wk), 1) >> 7
    rb_a = lax.broadcasted_iota(jnp.int32, (wa, wa), 0) >> 6
    lb_a = lax.broadcasted_iota(jnp.int32, (wa, wa), 1) >> 6
    rb_k = lax.broadcasted_iota(jnp.int32, (wa, wk), 0) >> 6
    lb_k = lax.broadcasted_iota(jnp.int32, (wa, wk), 1) >> 7
    eye = jnp.where(ri == ci, 1.0, 0.0)
    ri1 = lax.broadcasted_iota(jnp.int32, (c, c), 0)
    ci1 = lax.broadcasted_iota(jnp.int32, (c, c), 1)

    def bd_a(x):
        return _block_diag(x, rb_a, lb_a, pack)

    def bd_k(x):
        return _block_diag(x, rb_k, lb_k, pack)

    def per_head(cols, blk):
        out = cols[0]
        for t in range(1, pack):
            out = jnp.where(blk == t, cols[t], out)
        return out

    incl = [(ri >= ci), (ri <= ci)]
    strict = [(ri > ci), (ri < ci)]
    last = [c - 1, 0]
    levels = [[strict[d] & ((ri >> (k + 1)) == (ci >> (k + 1))) & ((ri >> k) != (ci >> k)) for k in range(6)]
              for d in range(2)]
    ones = [jnp.where(ri1 >= ci1, 1.0, 0.0).astype(BF16), jnp.where(ri1 <= ci1, 1.0, 0.0).astype(BF16)]

    packs = []
    for j in range(cpb):
        rows = slice(j * c, (j + 1) * c)
        gall = g_ref[0, rows]
        g1 = gall.astype(BF16)
        rem = gall - g1.astype(F32)
        g2 = rem.astype(BF16)
        g3 = (rem - g2.astype(F32)).astype(BF16)
        for d in range(2):
            gcum = _dot(ones[d], g1) + (_dot(ones[d], g2) + _dot(ones[d], g3))
            gt_ref[d, j] = jnp.exp(gcum[last[d]:last[d] + 1, :])
            for h0 in range(0, heads, pack):
                cols = [d * heads + h0 + t for t in range(pack)]
                gcs = [gcum[:, cc:cc + 1] for cc in cols]
                gc_a = per_head([jnp.broadcast_to(x, (c, wa)) for x in gcs], blk_a)
                gr_a = jnp.sum(jnp.where(ri == ci, gc_a, 0.0), axis=0, keepdims=True)
                decay = jnp.where(incl[d], jnp.exp(jnp.where(incl[d], gc_a - gr_a, 0.0)), 0.0)
                beta_k = per_head([jnp.broadcast_to(gall[:, 2 * heads + cc:2 * heads + cc + 1], (c, wk))
                                   for cc in cols], blk_k)
                gc_k = per_head([jnp.broadcast_to(x, (c, wk)) for x in gcs], blk_k)
                gl_k = per_head([jnp.broadcast_to(x[last[d]:last[d] + 1], (c, wk)) for x in gcs], blk_k)
                sl = slice(h0 * LANES, (h0 + pack) * LANES)
                q_k, k_k, v_k = q_ref[0, rows, sl], k_ref[0, rows, sl], v_ref[0, rows, sl]
                kb_k = k_k * beta_k
                eg_k = jnp.exp(gc_k)
                packs.append(dict(j=j, d=d, h0=h0, decay=decay, kb=kb_k, q=q_k, k=k_k, eg=eg_k,
                                  rhs_u=v_k * beta_k, rhs_w=kb_k * eg_k, kd=k_k * jnp.exp(gl_k - gc_k)))

    for p in packs:
        lhs = jnp.concatenate([p["kb"], p["q"]], axis=0).astype(BF16)
        p["aa"] = _dot_nt(lhs, bd_k(p["k"]))
    for p in packs:
        d = p["d"]
        p["a"] = jnp.where(strict[d], p["aa"][:c] * p["decay"], 0.0).astype(BF16).astype(F32)
        p["a_qk"] = jnp.where(incl[d], p["aa"][c:] * p["decay"], 0.0)
        p["t"] = eye - jnp.where(levels[d][0], p["a"], 0.0)
    for k in range(1, 6):
        for p in packs:
            p["ts"] = _split2(p["t"])
            p["x"] = _dot_hl(p["ts"], bd_a(jnp.where(levels[p["d"]][k], p["a"], 0.0)))
        for p in packs:
            xs = _split2(p["x"])
            y = _dot_hl(xs, bd_a(p["ts"][0].astype(F32))) + _dot(xs[0], bd_a(p["ts"][1].astype(F32)))
            p["t"] = p["t"] - y
    for p in packs:
        p["ts"] = _split2(p["t"])
    for name in ("u", "w"):
        for p in packs:
            rh, rl = _split2(p["rhs_" + name])
            p[name] = _dot_hl(p["ts"], bd_k(rh.astype(F32))) + _dot(p["ts"][0], bd_k(rl.astype(F32)))
    for p in packs:
        j, d = p["j"], p["d"]
        qd_k = p["q"] * p["eg"]
        for t in range(pack):
            h = p["h0"] + t
            ks = slice(t * LANES, (t + 1) * LANES)
            u_ref[d, j, h] = p["u"][:, ks]
            wq_ref[d, j, h, 0:c] = p["w"][:, ks].astype(BF16)
            wq_ref[d, j, h, c:2 * c] = qd_k[:, ks].astype(BF16)
            ak_ref[d, j, h, 0:c] = p["a_qk"][:, t * c:(t + 1) * c].astype(BF16)
            ak_ref[d, j, h, c:c + DN_DK] = p["kd"][:, ks].T.astype(BF16)


def _delta_scan_kernel(uf, wqf, akf, gtf, ub, wqb, akb, gtb, of, ob, state, *, heads):
    c = DN_CHUNK

    @pl.when(pl.program_id(1) == 0)
    def _():
        state[...] = jnp.zeros_like(state)

    dirs = ((uf, wqf, akf, gtf, of), (ub, wqb, akb, gtb, ob))
    chains = [(d, h) for d in range(2) for h in range(heads)]
    s_prev = {ch: state[ch[0], ch[1]] for ch in chains}
    r1 = {(d, h): _dot(dirs[d][1][h], s_prev[(d, h)].astype(BF16)) for d, h in chains}
    r2 = {}
    for d, h in chains:
        v_new = dirs[d][0][h] - r1[(d, h)][:c]
        r2[(d, h)] = _dot(dirs[d][2][h], v_new.astype(BF16))
    for d, h in chains:
        gt = dirs[d][3][...]
        dirs[d][4][0, :, h * LANES:(h + 1) * LANES] = r1[(d, h)][c:] + r2[(d, h)][:c]
        state[d, h] = s_prev[(d, h)] * gt[:, d * heads + h:d * heads + h + 1] + r2[(d, h)][c:]


def _delta(q, k, v, gates, n_lat, heads):
    b, s, vw = q.shape
    c = DN_CHUNK
    n_chunks, n_lat_c = s // c, n_lat // c
    n_ctx_c = n_chunks - n_lat_c
    cpb = _pick(n_chunks, (DN_CHUNKS_PER_STEP, 1))
    seq = pl.BlockSpec((1, cpb * c, vw), lambda b_, i: (b_, i, 0))

    def loc(rows, cols):
        return pl.BlockSpec((2, None, cpb, heads, rows, cols), lambda b_, i: (0, b_, i, 0, 0, 0))

    u, wq, ak, gt = pl.pallas_call(
        functools.partial(_delta_local_kernel, heads=heads, pack=math.gcd(heads, DN_PACK), cpb=cpb),
        grid=(b, n_chunks // cpb),
        in_specs=[seq, seq, seq, pl.BlockSpec((1, cpb * c, LANES), lambda b_, i: (b_, i, 0))],
        out_specs=[loc(c, DN_DV), loc(2 * c, DN_DV), loc(c + DN_DK, c),
                   pl.BlockSpec((2, None, cpb, 1, LANES), lambda b_, i: (0, b_, i, 0, 0))],
        out_shape=[jax.ShapeDtypeStruct((2, b, n_chunks, heads, c, DN_DV), F32),
                   jax.ShapeDtypeStruct((2, b, n_chunks, heads, 2 * c, DN_DV), BF16),
                   jax.ShapeDtypeStruct((2, b, n_chunks, heads, c + DN_DK, c), BF16),
                   jax.ShapeDtypeStruct((2, b, n_chunks, 1, LANES), F32)],
        compiler_params=_params("arbitrary", "arbitrary"), name="delta_local",
    )(q, k, v, gates)

    def fwd(st):
        return jnp.where(st < n_ctx_c, n_lat_c + st, st - n_ctx_c)

    def bwd(st):
        return jnp.where(st < n_ctx_c, n_chunks - 1 - st, n_lat_c - 1 - (st - n_ctx_c))

    def chunk(d, order, rows, cols):
        return pl.BlockSpec((None, None, None, heads, rows, cols), lambda b_, st: (d, b_, order(st), 0, 0, 0))

    def gspec(d, order):
        return pl.BlockSpec((None, None, None, 1, LANES), lambda b_, st: (d, b_, order(st), 0, 0))

    def ospec(order):
        return pl.BlockSpec((1, c, vw), lambda b_, st: (b_, order(st), 0))

    ins, args = [], []
    for d, order in ((0, fwd), (1, bwd)):
        ins += [chunk(d, order, c, DN_DV), chunk(d, order, 2 * c, DN_DV), chunk(d, order, c + DN_DK, c),
                gspec(d, order)]
        args += [u, wq, ak, gt]
    out = jax.ShapeDtypeStruct((b, s, vw), F32)
    return pl.pallas_call(
        functools.partial(_delta_scan_kernel, heads=heads), grid=(b, n_chunks),
        in_specs=ins, out_specs=[ospec(fwd), ospec(bwd)], out_shape=[out, out],
        scratch_shapes=[pltpu.VMEM((2, heads, DN_DK, DN_DV), F32)],
        compiler_params=_params("arbitrary", "arbitrary"), name="delta_scan",
    )(*args)


def _lru_kernel(uf_ref, ub_ref, wri_ref, bri_ref, spl_ref, hf_ref, hb_ref, carry, *, r, n_blk):
    @pl.when(pl.program_id(1) == 0)
    def _():
        carry[...] = jnp.zeros_like(carry)

    row = lax.broadcasted_iota(jnp.int32, (r, 1), 0)
    for d, (u_ref, h_ref) in enumerate(((uf_ref, hf_ref), (ub_ref, hb_ref))):
        for n in range(n_blk):
            sl = slice(n * LANES, (n + 1) * LANES)
            u = u_ref[0, :, sl]
            ri = _dot(u.astype(BF16), wri_ref[d, n]) + bri_ref[d, n]
            rg = _sigmoid(ri[:, :LANES])
            ig = _sigmoid(ri[:, LANES:])
            log_a = -spl_ref[d, n] * rg
            a = jnp.exp(log_a)
            bt = jnp.sqrt(-_expm1_nonpos(2.0 * log_a)) * (ig * u)
            sh = 1
            while sh < r:
                if d == 0:
                    keep = row >= sh
                    a_s, b_s = pltpu.roll(a, sh, 0), pltpu.roll(bt, sh, 0)
                else:
                    keep = row < r - sh
                    a_s, b_s = pltpu.roll(a, r - sh, 0), pltpu.roll(bt, r - sh, 0)
                bt = a * jnp.where(keep, b_s, 0.0) + bt
                a = a * jnp.where(keep, a_s, 1.0)
                sh *= 2
            h = bt + a * carry[d, :, sl]
            h_ref[0, :, sl] = h
            carry[d, :, sl] = h[r - 1:r] if d == 0 else h[0:1]


def _lru(u, w_ri, b_ri, spl, r, n_lat):
    b, s, width = u.shape
    n_blk = width // LANES
    nb, nlb = s // r, n_lat // r
    ncb = nb - nlb

    def fwd(st):
        return jnp.where(st < ncb, nlb + st, st - ncb)

    def bwd(st):
        return jnp.where(st < ncb, nb - 1 - st, nlb - 1 - (st - ncb))

    def seq(order):
        return pl.BlockSpec((1, r, width), lambda b_, st: (b_, order(st), 0))

    def par(shape):
        return pl.BlockSpec(shape, lambda b_, st: (0,) * len(shape))

    out = jax.ShapeDtypeStruct((b, s, width), F32)
    return pl.pallas_call(
        functools.partial(_lru_kernel, r=r, n_blk=n_blk), grid=(b, nb),
        in_specs=[seq(fwd), seq(bwd), par(w_ri.shape), par(b_ri.shape), par(spl.shape)],
        out_specs=[seq(fwd), seq(bwd)], out_shape=[out, out],
        scratch_shapes=[pltpu.VMEM((2, 1, width), F32)],
        compiler_params=_params("arbitrary", "arbitrary"), name="lru",
    )(u, u, w_ri, b_ri, spl)


def _gelu_tanh(x):
    return 0.5 * x * (1.0 + jnp.tanh(math.sqrt(2.0 / math.pi) * (x + 0.044715 * (x * x * x))))


def _even_out_kernel(of_ref, ob_ref, z_ref, hf_ref, hb_ref, y_ref, on_ref, mix_ref, *, heads, n_blk):
    for h in range(heads):
        sl = slice(h * LANES, (h + 1) * LANES)
        o = of_ref[0, :, sl] + ob_ref[0, :, sl]
        o = o * lax.rsqrt(jnp.mean(o * o, axis=-1, keepdims=True) + NORM_EPS) * on_ref[...]
        mix_ref[0, :, sl] = (o * _silu(z_ref[0, :, sl])).astype(BF16)
    for n in range(n_blk):
        sl = slice(n * LANES, (n + 1) * LANES)
        osl = slice((heads + n) * LANES, (heads + n + 1) * LANES)
        mix_ref[0, :, osl] = ((hf_ref[0, :, sl] + hb_ref[0, :, sl]) * _gelu_tanh(y_ref[0, :, sl])).astype(BF16)


def _even_out(o_f, o_b, p, h_f, h_b, o_norm, r, heads):
    b, s, vw = o_f.shape
    width = h_f.shape[2]

    def cur(wd, cb):
        return pl.BlockSpec((1, r, wd), lambda b_, i: (b_, i, cb))

    zcb = (3 * vw) // vw
    ycb = (4 * vw + width) // width
    return pl.pallas_call(
        functools.partial(_even_out_kernel, heads=heads, n_blk=width // LANES), grid=(b, s // r),
        in_specs=[cur(vw, 0), cur(vw, 0), cur(vw, zcb), cur(width, 0), cur(width, 0), cur(width, ycb),
                  pl.BlockSpec((1, DN_DV), lambda b_, i: (0, 0))],
        out_specs=cur(vw + width, 0), out_shape=jax.ShapeDtypeStruct((b, s, vw + width), BF16),
        compiler_params=_params("arbitrary", "arbitrary"), name="even_out",
    )(o_f, o_b, p, h_f, h_b, p, o_norm.reshape(1, DN_DV))


def _rope_pair(y, tab):
    y = y * tab
    return y + pltpu.roll(y, MLA_ROPE, 1)


def _q_proj_kernel(a_ref, w_ref, tab_ref, q_ref, *, scale):
    acc = _dot(a_ref[0], w_ref[0])
    q_ref[0, 0, :, 0:MLA_NOPE] = (acc[:, :MLA_NOPE] * scale).astype(BF16)
    qr = _rope_pair(acc[:, MLA_NOPE:], tab_ref[...])
    q_ref[0, 0, :, MLA_NOPE:MLA_QD] = (qr[:, :MLA_ROPE] * scale).astype(BF16)


def _q_proj(h, w_q, tab, heads):
    b, s, d = h.shape
    tm = _pick(s, (768, 512, 640, 256, 128))
    return pl.pallas_call(
        functools.partial(_q_proj_kernel, scale=math.log2(math.e) * MLA_QD ** -0.5), grid=(b, s // tm, heads),
        in_specs=[pl.BlockSpec((1, tm, d), lambda b_, i, hd: (b_, i, 0)),
                  pl.BlockSpec((1, d, 2 * LANES), lambda b_, i, hd: (hd, 0, 0)),
                  pl.BlockSpec((tm, LANES), lambda b_, i, hd: (i, 0))],
        out_specs=pl.BlockSpec((1, 1, tm, MLA_QD), lambda b_, i, hd: (b_, hd, i, 0)),
        out_shape=jax.ShapeDtypeStruct((b, heads, s, MLA_QD), BF16),
        compiler_params=_params("arbitrary", "arbitrary", "arbitrary"), name="q_proj",
    )(h, w_q, tab)


def _ckv_proj_kernel(a_ref, w_ref, tab_ref, g_ref, ckv_ref, kr_ref, *, rank):
    acc = _dot(a_ref[0], w_ref[...])
    ckv = acc[:, :rank]
    ckv = ckv * lax.rsqrt(jnp.mean(ckv * ckv, axis=-1, keepdims=True) + NORM_EPS) * g_ref[...]
    ckv_ref[0] = ckv.astype(BF16)
    kr_ref[0] = _rope_pair(acc[:, rank:], tab_ref[...]).astype(BF16)


def _ckv_proj(h, w_c, tab, kv_norm):
    b, s, d = h.shape
    rank = kv_norm.shape[0]
    tm = _pick(s, (768, 512, 640, 256, 128))
    return pl.pallas_call(
        functools.partial(_ckv_proj_kernel, rank=rank), grid=(b, s // tm),
        in_specs=[pl.BlockSpec((1, tm, d), lambda b_, i: (b_, i, 0)),
                  pl.BlockSpec((d, rank + LANES), lambda b_, i: (0, 0)),
                  pl.BlockSpec((tm, LANES), lambda b_, i: (i, 0)),
                  pl.BlockSpec((1, rank), lambda b_, i: (0, 0))],
        out_specs=[pl.BlockSpec((1, tm, rank), lambda b_, i: (b_, i, 0)),
                   pl.BlockSpec((1, tm, LANES), lambda b_, i: (b_, i, 0))],
        out_shape=[jax.ShapeDtypeStruct((b, s, rank), BF16), jax.ShapeDtypeStruct((b, s, LANES), BF16)],
        compiler_params=_params("arbitrary", "arbitrary"), name="ckv_proj",
    )(h, w_c, tab, kv_norm.reshape(1, rank))


def _kv_up_kernel(a_ref, w_ref, kr_ref, k_ref, v_ref):
    acc = _dot(a_ref[0], w_ref[0])
    k_ref[0, 0, :, 0:MLA_NOPE] = acc[:, :MLA_NOPE].astype(BF16)
    k_ref[0, 0, :, MLA_NOPE:MLA_QD] = kr_ref[0, :, 0:MLA_ROPE]
    v_ref[0, 0] = acc[:, MLA_NOPE:].astype(BF16)


def _kv_up(ckv, w_ukv, kr, heads):
    b, s, rank = ckv.shape
    tm = _pick(s, (768, 512, 640, 256, 128))
    return pl.pallas_call(
        _kv_up_kernel, grid=(b, s // tm, heads),
        in_specs=[pl.BlockSpec((1, tm, rank), lambda b_, i, hd: (b_, i, 0)),
                  pl.BlockSpec((1, rank, MLA_NOPE + MLA_V), lambda b_, i, hd: (hd, 0, 0)),
                  pl.BlockSpec((1, tm, LANES), lambda b_, i, hd: (b_, i, 0))],
        out_specs=[pl.BlockSpec((1, 1, tm, MLA_QD), lambda b_, i, hd: (b_, hd, i, 0)),
                   pl.BlockSpec((1, 1, tm, MLA_V), lambda b_, i, hd: (b_, hd, i, 0))],
        out_shape=[jax.ShapeDtypeStruct((b, heads, s, MLA_QD), BF16),
                   jax.ShapeDtypeStruct((b, heads, s, MLA_V), BF16)],
        compiler_params=_params("arbitrary", "arbitrary", "arbitrary"), name="kv_up",
    )(ckv, w_ukv, kr)


def _flash_kernel(q_ref, k_ref, v_ref, o_ref, m_ref, l_ref, acc_ref, *, sub):
    j = pl.program_id(3)

    @pl.when(j == 0)
    def _():
        m_ref[...] = jnp.full_like(m_ref, -jnp.inf)
        l_ref[...] = jnp.zeros_like(l_ref)
        acc_ref[...] = jnp.zeros_like(acc_ref)

    k = k_ref[0, 0]
    v = v_ref[0, 0]
    for c in range(q_ref.shape[2] // sub):
        rows = slice(c * sub, (c + 1) * sub)
        s = _dot_nt(q_ref[0, 0, rows, :], k)
        m_prev = m_ref[rows]
        m_new = jnp.maximum(m_prev, jnp.max(s, axis=-1, keepdims=True))
        p = jnp.exp2(s - m_new)
        alpha = jnp.exp2(m_prev - m_new)
        l_ref[rows] = alpha * l_ref[rows] + jnp.sum(p, axis=-1, keepdims=True)
        acc_ref[rows] = alpha * acc_ref[rows] + _dot(p.astype(BF16), v)
        m_ref[rows] = m_new

    @pl.when(j == pl.num_programs(3) - 1)
    def _():
        o_ref[0] = (acc_ref[...] / l_ref[...]).astype(o_ref.dtype)


def _attn_full_kernel(q_ref, k_ref, v_ref, o_ref, *, sub):
    k = k_ref[0, 0]
    v = v_ref[0, 0]
    for c in range(q_ref.shape[2] // sub):
        rows = slice(c * sub, (c + 1) * sub)
        s = _dot_nt(q_ref[0, 0, rows, :], k)
        p = jnp.exp2(s - jnp.max(s, axis=-1, keepdims=True))
        l = jnp.sum(p, axis=-1, keepdims=True)
        o_ref[0, rows, :] = (_dot(p.astype(BF16), v) / l).astype(o_ref.dtype)


def _flash(q, k, v, q_rows, kv_rows):
    b, heads, _, dq = q.shape
    dv = v.shape[3]
    q0, nq = q_rows
    k0, nk = kv_rows
    tq = _pick(math.gcd(nq, q0) if q0 else nq, (2048, 1024, 512, 256, 128))
    tk = _pick(math.gcd(nk, k0) if k0 else nk, (nk, 768, 512, 256, 128))
    qo, ko = q0 // tq, k0 // tk
    if nk == tk:
        return pl.pallas_call(
            functools.partial(_attn_full_kernel, sub=min(tq, FLASH_SUB)), grid=(b, heads, nq // tq),
            in_specs=[pl.BlockSpec((1, 1, tq, dq), lambda b_, h, i: (b_, h, qo + i, 0)),
                      pl.BlockSpec((1, 1, tk, dq), lambda b_, h, i: (b_, h, ko, 0)),
                      pl.BlockSpec((1, 1, tk, dv), lambda b_, h, i: (b_, h, ko, 0))],
            out_specs=pl.BlockSpec((1, tq, dv), lambda b_, h, i: (b_, i, h)),
            out_shape=jax.ShapeDtypeStruct((b, nq, heads * dv), BF16),
            compiler_params=_params("arbitrary", "arbitrary", "arbitrary"), name="attn_full",
        )(q, k, v)
    return pl.pallas_call(
        functools.partial(_flash_kernel, sub=min(tq, FLASH_SUB)), grid=(b, heads, nq // tq, nk // tk),
        in_specs=[pl.BlockSpec((1, 1, tq, dq), lambda b_, h, i, j: (b_, h, qo + i, 0)),
                  pl.BlockSpec((1, 1, tk, dq), lambda b_, h, i, j: (b_, h, ko + j, 0)),
                  pl.BlockSpec((1, 1, tk, dv), lambda b_, h, i, j: (b_, h, ko + j, 0))],
        out_specs=pl.BlockSpec((1, tq, dv), lambda b_, h, i, j: (b_, i, h)),
        out_shape=jax.ShapeDtypeStruct((b, nq, heads * dv), BF16),
        scratch_shapes=[pltpu.VMEM((tq, 1), F32), pltpu.VMEM((tq, 1), F32), pltpu.VMEM((tq, dv), F32)],
        compiler_params=_params("arbitrary", "arbitrary", "arbitrary", "arbitrary"), name="flash",
    )(q, k, v)


def _experts_kernel(be_ref, nu_ref, x_ref, w1_ref, w3_ref, w2_ref, y_ref):
    @pl.when(pl.program_id(0) < nu_ref[0])
    def _():
        x = x_ref[...].astype(BF16)
        h1 = _dot(x, w1_ref[0].astype(BF16))
        h3 = _dot(x, w3_ref[0].astype(BF16))
        hh = (_silu(h1) * h3).astype(BF16)
        y_ref[...] = _dot(hh, w2_ref[0].astype(BF16)).astype(y_ref.dtype)


def _experts(xg, blk_e, n_used, w1, w3, w2, layer, tm):
    rows, d = xg.shape
    de = w1.shape[3]
    grid_spec = pltpu.PrefetchScalarGridSpec(
        num_scalar_prefetch=2, grid=(rows // tm,),
        in_specs=[pl.BlockSpec((tm, d), lambda i, be, nu: (i, 0)),
                  pl.BlockSpec((None, 1, d, de), lambda i, be, nu: (layer, be[i], 0, 0)),
                  pl.BlockSpec((None, 1, d, de), lambda i, be, nu: (layer, be[i], 0, 0)),
                  pl.BlockSpec((None, 1, de, d), lambda i, be, nu: (layer, be[i], 0, 0))],
        out_specs=pl.BlockSpec((tm, d), lambda i, be, nu: (i, 0)))
    return pl.pallas_call(
        _experts_kernel, grid_spec=grid_spec, out_shape=jax.ShapeDtypeStruct((rows, d), BF16),
        compiler_params=_params("arbitrary"), name="experts",
    )(blk_e, n_used, xg, w1, w3, w2)


def _moe(h, logits, n_groups, n_experts, w1, w3, w2, layer, tm):
    n_tok, d = h.shape
    epg = n_experts // n_groups
    lg = logits[:, :n_groups]
    grp = jnp.argmax(lg, axis=-1)
    p_grp = jnp.take_along_axis(jax.nn.softmax(lg, -1), grp[:, None], -1)
    le = logits[:, n_groups:n_groups + n_experts].reshape(n_tok, n_groups, epg)
    le = jnp.take_along_axis(le, grp[:, None, None], axis=1)[:, 0]
    top_v, top_i = lax.top_k(le, TOP_K)
    gate = p_grp * jax.nn.softmax(top_v, -1)
    expert = (grp[:, None] * epg + top_i).reshape(-1).astype(jnp.int32)
    onehot = (expert[:, None] == jnp.arange(n_experts, dtype=jnp.int32)[None, :]).astype(jnp.int32)
    rank = jnp.sum((jnp.cumsum(onehot, axis=0) - onehot) * onehot, axis=1)
    counts = jnp.sum(onehot, axis=0)
    padded = (counts + tm - 1) // tm * tm
    pend = jnp.cumsum(padded)
    dest = (pend - padded)[expert] + rank
    n_blk = (n_tok * TOP_K + tm - 1) // tm + n_experts
    blk_e = jnp.minimum(jnp.sum(pend[None, :] <= (jnp.arange(n_blk) * tm)[:, None], axis=1), n_experts - 1)
    n_used = (pend[-1] // tm).reshape(1)
    src = (jnp.arange(n_blk * tm, dtype=jnp.int32) % n_tok).at[dest].set(
        jnp.arange(n_tok * TOP_K, dtype=jnp.int32) // TOP_K)
    y = _experts(h[src], blk_e.astype(jnp.int32), n_used.astype(jnp.int32), w1, w3, w2, layer, tm)
    dest = dest.reshape(n_tok, TOP_K)
    return (y[dest[:, 0]], y[dest[:, 1]]), (gate[:, 0:1], gate[:, 1:2])


def _rope_table(n_lat, n_ctx):
    rows = n_lat // GRID_W
    row = jnp.broadcast_to(jnp.arange(rows)[:, None], (rows, GRID_W)).reshape(-1)
    col = jnp.broadcast_to(jnp.arange(GRID_W)[None, :], (rows, GRID_W)).reshape(-1)
    pos = jnp.stack([row, col], -1).astype(F32)
    inv = ROPE_BASE ** (-jnp.arange(ROPE_FREQS, dtype=F32) / ROPE_FREQS)
    ang = pos[:, :, None] * inv
    ang = jnp.broadcast_to(ang[:, :, None, :], (n_lat, 2, 2, ROPE_FREQS)).reshape(n_lat, MLA_ROPE)
    lat = jnp.concatenate([jnp.cos(ang), jnp.sin(ang)], axis=-1)
    ctx = jnp.concatenate([jnp.ones((n_ctx, MLA_ROPE), F32), jnp.zeros((n_ctx, MLA_ROPE), F32)], axis=-1)
    return jnp.concatenate([lat, ctx], axis=0)


def _rot_cols(w):
    wr = w.reshape(w.shape[:-1] + (2, 2, ROPE_FREQS))
    return jnp.stack([-wr[..., 1, :], wr[..., 0, :]], axis=-2).reshape(w.shape)


def kernel(x, c, ctx, c_ctx, ada_w, ada_b, ln_mix_g, ln_mix_b, ln_ffn_g, ln_ffn_b, ev_w_in, ev_conv_qkv, ev_a_log,
           ev_dt_bias, ev_o_norm, ev_conv_x_w, ev_conv_x_b, ev_w_r, ev_b_r, ev_w_i, ev_b_i, ev_lam, ev_w_out,
           od_w_in, od_kv_norm, od_w_ukv, od_w_out, moe_w_grp, moe_b_grp, moe_w_exp, moe_b_exp, moe_w1, moe_w3,
           moe_w2):
    bsz, n_lat, d = x.shape
    n_ctx = ctx.shape[1]
    s = n_lat + n_ctx
    depth = ada_w.shape[0]
    alpha = (2.0 * depth) ** 0.25
    r = min(256, n_ctx)
    assert n_lat % r == 0 and n_ctx % r == 0 and n_lat % GRID_W == 0 and bsz + 1 <= 8
    dn_heads = ev_a_log.shape[-1]
    vw = dn_heads * DN_DV
    width = ev_lam.shape[-1]
    lru_blocks = ev_w_r.shape[2]
    assert width // lru_blocks == LANES
    rank = od_kv_norm.shape[-1]
    mla_heads = od_w_ukv.shape[-1] // (MLA_NOPE + MLA_V)
    n_groups, n_experts = moe_w_grp.shape[-1], moe_w_exp.shape[-1]
    moe_tm = 256 if (bsz * s * TOP_K) // n_experts >= 512 else 128

    xs = jnp.concatenate([x, ctx], axis=1)
    c_all = jnp.zeros((8, d), F32).at[:bsz].set(c).at[bsz].set(c_ctx)
    mods = _adaln(c_all, ada_w, ada_b).reshape(depth, 8, 6, d)

    def seg_tab(layer, k):
        lat = mods[layer, :bsz, k]
        ctx_v = jnp.broadcast_to(mods[layer, bsz, k][None], (bsz, d))
        return jnp.stack([lat, ctx_v], axis=1)[:, :, None, :]

    rope_tab = _rope_table(n_lat, n_ctx)
    hmod = _modulate(xs, seg_tab(0, 0), seg_tab(0, 1), r, n_lat)
    for layer in range(depth):
        i = layer // 2
        last = layer == depth - 1
        if layer % 2 == 0:
            qkvw = 3 * vw
            w = ev_w_in[i]
            o_z, o_a, o_b, o_x, o_y = qkvw, qkvw + vw, qkvw + vw + 2 * dn_heads, qkvw + vw + 4 * dn_heads, \
                qkvw + vw + 4 * dn_heads + width
            w_perm = jnp.concatenate([w[:, :o_a], w[:, o_x:], w[:, o_a:o_x],
                                      jnp.zeros((d, LANES - 4 * dn_heads), F32)], axis=1).astype(BF16)
            p = _mm(hmod.reshape(bsz * s, d), w_perm).reshape(bsz, s, -1)
            q, k, v, u, gb = _even_prep(p, ev_conv_qkv[i], ev_conv_x_w[i], ev_conv_x_b[i], -jnp.exp(ev_a_log[i]),
                                        ev_dt_bias[i], r, n_lat, dn_heads, width)
            o_f, o_b = _delta(q, k, v, gb, n_lat, dn_heads)
            w_ri = jnp.concatenate([ev_w_r[i], ev_w_i[i]], axis=-1).astype(BF16)
            b_ri = jnp.concatenate([ev_b_r[i].reshape(2, lru_blocks, 1, LANES),
                                    ev_b_i[i].reshape(2, lru_blocks, 1, LANES)], axis=-1)
            spl = (LRU_C * jax.nn.softplus(-ev_lam[i])).reshape(2, lru_blocks, 1, LANES)
            h_f, h_b = _lru(u, w_ri, b_ri, spl, r, n_lat)
            mix = _even_out(o_f, o_b, p, h_f, h_b, ev_o_norm[i], r, dn_heads)
            w_out = ev_w_out[i]
        else:
            w = od_w_in[i]
            nq = mla_heads * MLA_QD
            wq = w[:, :nq].reshape(d, mla_heads, MLA_QD)
            wq = jnp.concatenate([wq, _rot_cols(wq[..., MLA_NOPE:])], axis=-1).transpose(1, 0, 2).astype(BF16)
            w_kr = w[:, nq + rank:]
            w_c = jnp.concatenate([w[:, nq:nq + rank], w_kr, _rot_cols(w_kr)], axis=-1).astype(BF16)
            w_u = od_w_ukv[i].reshape(rank, mla_heads, MLA_NOPE + MLA_V).transpose(1, 0, 2).astype(BF16)
            qh = _q_proj(hmod, wq, rope_tab, mla_heads)
            ckv, kr = _ckv_proj(hmod, w_c, rope_tab, od_kv_norm[i])
            kh, vh = _kv_up(ckv, w_u, kr, mla_heads)
            mix = jnp.concatenate([_flash(qh, kh, vh, (0, n_lat), (0, s)),
                                   _flash(qh, kh, vh, (n_lat, n_ctx), (n_lat, n_ctx))], axis=1)
            w_out = od_w_out[i]
        y = _mm(mix.reshape(bsz * s, -1), w_out.astype(BF16), BF16).reshape(bsz, s, d)
        nr = LANES * ((n_groups + n_experts + LANES - 1) // LANES)
        w_rt = jnp.pad(jnp.concatenate([moe_w_grp[layer], moe_w_exp[layer]], axis=1),
                       ((0, 0), (0, nr - n_groups - n_experts)))
        b_rt = jnp.pad(jnp.concatenate([moe_b_grp[layer], moe_b_exp[layer]]), (0, nr - n_groups - n_experts))
        xs, hf, logits = _postnorm(xs, [y], seg_tab(layer, 2), ln_mix_g[layer], ln_mix_b[layer], r, n_lat, alpha,
                                   shift=seg_tab(layer, 3), scale=seg_tab(layer, 4), router=(w_rt, b_rt[None]))
        ys, gts = _moe(hf.reshape(bsz * s, d), logits.reshape(bsz * s, nr), n_groups, n_experts,
                       moe_w1, moe_w3, moe_w2, layer, moe_tm)
        ys = [t.reshape(bsz, s, d) for t in ys]
        gts = [t.reshape(bsz, s, 1) for t in gts]
        if last:
            (xs,) = _postnorm(xs, ys, seg_tab(layer, 5), ln_ffn_g[layer], ln_ffn_b[layer], r, n_lat, alpha,
                              rows=n_lat, row_gates=gts)
        else:
            xs, hmod = _postnorm(xs, ys, seg_tab(layer, 5), ln_ffn_g[layer], ln_ffn_b[layer], r, n_lat, alpha,
                                 shift=seg_tab(layer + 1, 0), scale=seg_tab(layer + 1, 1), row_gates=gts)
    return xs
```

```python
import functools
import math

import jax
import jax.numpy as jnp
from jax import lax
from jax.experimental import pallas as pl
from jax.experimental.pallas import tpu as pltpu

F32 = jnp.float32
BF16 = jnp.bfloat16
HIGHEST = lax.Precision.HIGHEST

LANES = 128
DN_DK = 128
DN_DV = 128
DN_CHUNK = 64
MLA_NOPE = 128
MLA_ROPE = 64
MLA_V = 128
MLA_QD = MLA_NOPE + MLA_ROPE
GRID_W = 64
ROPE_FREQS = MLA_ROPE // 4
ROPE_BASE = 10000.0
LRU_C = 8.0
LN_EPS = 1e-5
NORM_EPS = 1e-6
TOP_K = 2
FLASH_SUB = 256
VMEM_LIMIT = 48 * 1024 * 1024


def _pick(n, cands):
    for c in cands:
        if n % c == 0:
            return c
    raise ValueError(f"no tile for {n} in {cands}")


def _params(*sem):
    return pltpu.CompilerParams(dimension_semantics=sem, vmem_limit_bytes=VMEM_LIMIT)


def _dot(a, b, precision=None):
    return jnp.dot(a, b, preferred_element_type=F32, precision=precision)


def _dot_nt(a, b, precision=None):
    return lax.dot_general(a, b, (((1,), (1,)), ((), ())), preferred_element_type=F32, precision=precision)


def _sigmoid(x):
    return 1.0 / (1.0 + jnp.exp(-x))


def _silu(x):
    return x * _sigmoid(x)


def _softplus(x):
    return jnp.maximum(x, 0.0) + jnp.log1p(jnp.exp(-jnp.abs(x)))


def _expm1_nonpos(x):
    u = jnp.exp(x)
    safe = (x > -0.5) & (u < 1.0)
    stable = (u - 1.0) * x / jnp.log(jnp.where(safe, u, 0.5))
    return jnp.where(safe, stable, jnp.where(u < 1.0, u - 1.0, x))


def _adaln_kernel(c_ref, w_ref, b_ref, o_ref):
    a = _silu(c_ref[...])
    o_ref[0] = _dot(a.astype(BF16), w_ref[0].astype(BF16)) + b_ref[0]


def _adaln(c_all, ada_w, ada_b):
    n_layer, d, n6 = ada_w.shape
    tn = _pick(n6, (1024, 512, 256, 128))
    return pl.pallas_call(
        _adaln_kernel, grid=(n_layer, n6 // tn),
        in_specs=[pl.BlockSpec((8, d), lambda l, j: (0, 0)),
                  pl.BlockSpec((1, d, tn), lambda l, j: (l, 0, j)),
                  pl.BlockSpec((1, 1, tn), lambda l, j: (l, 0, j))],
        out_specs=pl.BlockSpec((1, 8, tn), lambda l, j: (l, 0, j)),
        out_shape=jax.ShapeDtypeStruct((n_layer, 8, n6), F32),
        compiler_params=_params("arbitrary", "arbitrary"), name="adaln",
    )(c_all, ada_w, ada_b.reshape(n_layer, 1, n6))


def _seg_spec(d, n_lat_blocks):
    return pl.BlockSpec((1, 1, 1, d), lambda b, i: (b, jnp.where(i >= n_lat_blocks, 1, 0), 0, 0))


def _modulate_kernel(x_ref, sh_ref, sc_ref, h_ref):
    h_ref[0] = (x_ref[0] * (1.0 + sc_ref[0, 0]) + sh_ref[0, 0]).astype(BF16)


def _modulate(x, shift, scale, r, n_lat):
    b, s, d = x.shape
    row = pl.BlockSpec((1, r, d), lambda b_, i: (b_, i, 0))
    return pl.pallas_call(
        _modulate_kernel, grid=(b, s // r),
        in_specs=[row, _seg_spec(d, n_lat // r), _seg_spec(d, n_lat // r)],
        out_specs=row, out_shape=jax.ShapeDtypeStruct((b, s, d), BF16),
        compiler_params=_params("arbitrary", "arbitrary"), name="modulate",
    )(x, shift, scale)


def _postnorm_kernel(*refs, alpha, n_y, row_gated, with_h, with_router):
    x_ref = refs[0]
    pos = 1
    y = None
    for _ in range(n_y):
        term = refs[pos][0].astype(F32)
        pos += 1
        if row_gated:
            term = term * refs[pos][0]
            pos += 1
        y = term if y is None else y + term
    gate_ref, g_ref, b_ref = refs[pos:pos + 3]
    pos += 3
    if with_h:
        sh_ref, sc_ref = refs[pos:pos + 2]
        pos += 2
    if with_router:
        wr_ref, br_ref = refs[pos:pos + 2]
        pos += 2
    xo_ref = refs[pos]
    v = alpha * x_ref[0] + gate_ref[0, 0] * y
    mu = jnp.mean(v, axis=-1, keepdims=True)
    vc = v - mu
    var = jnp.mean(vc * vc, axis=-1, keepdims=True)
    xn = vc * lax.rsqrt(var + LN_EPS) * g_ref[...] + b_ref[...]
    xo_ref[0] = xn
    if with_h:
        h = xn * (1.0 + sc_ref[0, 0]) + sh_ref[0, 0]
        refs[pos + 1][0] = h.astype(BF16)
        if with_router:
            refs[pos + 2][0] = _dot(h, wr_ref[...], HIGHEST) + br_ref[...]


def _postnorm(x, ys, gate, ln_g, ln_b, r, n_lat, alpha, shift=None, scale=None, router=None, rows=None,
              row_gates=None):
    b, s, d = x.shape
    rows = s if rows is None else rows
    nlb = n_lat // r
    row = pl.BlockSpec((1, r, d), lambda b_, i: (b_, i, 0))
    col1 = pl.BlockSpec((1, r, 1), lambda b_, i: (b_, i, 0))
    vec = pl.BlockSpec((1, d), lambda b_, i: (0, 0))
    args, in_specs = [x], [row]
    for j, y in enumerate(ys):
        args.append(y)
        in_specs.append(row)
        if row_gates is not None:
            args.append(row_gates[j])
            in_specs.append(col1)
    args += [gate, ln_g.reshape(1, d), ln_b.reshape(1, d)]
    in_specs += [_seg_spec(d, nlb), vec, vec]
    out_shape = [jax.ShapeDtypeStruct((b, rows, d), F32)]
    out_specs = [row]
    with_h = shift is not None
    if with_h:
        args += [shift, scale]
        in_specs += [_seg_spec(d, nlb), _seg_spec(d, nlb)]
        out_shape.append(jax.ShapeDtypeStruct((b, rows, d), BF16))
        out_specs.append(row)
    if router is not None:
        w_r, b_r = router
        nr = w_r.shape[1]
        args += [w_r, b_r]
        in_specs += [pl.BlockSpec((d, nr), lambda b_, i: (0, 0)), pl.BlockSpec((1, nr), lambda b_, i: (0, 0))]
        out_shape.append(jax.ShapeDtypeStruct((b, rows, nr), F32))
        out_specs.append(pl.BlockSpec((1, r, nr), lambda b_, i: (b_, i, 0)))
    return pl.pallas_call(
        functools.partial(_postnorm_kernel, alpha=alpha, n_y=len(ys), row_gated=row_gates is not None,
                          with_h=with_h, with_router=router is not None),
        grid=(b, rows // r), in_specs=in_specs, out_specs=out_specs, out_shape=out_shape,
        compiler_params=_params("arbitrary", "arbitrary"), name="postnorm",
    )(*args)


def _mm_kernel(a_ref, b_ref, o_ref):
    o_ref[...] = _dot(a_ref[...].astype(BF16), b_ref[...].astype(BF16)).astype(o_ref.dtype)


def _mm(a, b, out_dtype=F32):
    m, k = a.shape
    n = b.shape[1]
    tm = _pick(m, (512, 256, 128, 64, 8))
    tn = _pick(n, (1024, 896, 768, 640, 512, 384, 256, 128))
    return pl.pallas_call(
        _mm_kernel, grid=(m // tm, n // tn),
        in_specs=[pl.BlockSpec((tm, k), lambda i, j: (i, 0)), pl.BlockSpec((k, tn), lambda i, j: (0, j))],
        out_specs=pl.BlockSpec((tm, tn), lambda i, j: (i, j)),
        out_shape=jax.ShapeDtypeStruct((m, n), out_dtype),
        compiler_params=_params("arbitrary", "arbitrary"), name="mm",
    )(a, b)


def _even_prep_kernel(qkv_ref, qkv_p_ref, qkv_n_ref, xr_ref, xr_p_ref, xr_n_ref, ab_ref,
                      cw_ref, xw_ref, xb_ref, nea_ref, dtb_ref,
                      q_ref, k_ref, v_ref, u_ref, gb_ref, *, n_lat_blocks, n_blocks, heads, r):
    i = pl.program_id(1)
    pv = jnp.where((i != 0) & (i != n_lat_blocks), 1.0, 0.0)
    nv = jnp.where((i != n_lat_blocks - 1) & (i != n_blocks - 1), 1.0, 0.0)
    row = lax.broadcasted_iota(jnp.int32, (r, 1), 0)

    def conv(x, p8, n8, w):
        p8 = p8 * pv
        n8 = n8 * nv
        xm1 = jnp.where(row == 0, p8[7:8], pltpu.roll(x, 1, 0))
        xm2 = jnp.where(row == 0, p8[6:7], jnp.where(row == 1, p8[7:8], pltpu.roll(x, 2, 0)))
        xp1 = jnp.where(row == r - 1, n8[0:1], pltpu.roll(x, r - 1, 0))
        return w[0:1] * xm2 + w[1:2] * xm1 + w[2:3] * x + w[3:4] * xp1

    for j in range(3 * heads):
        sl = slice(j * LANES, (j + 1) * LANES)
        y = _silu(conv(qkv_ref[0, :, sl], qkv_p_ref[0, :, sl], qkv_n_ref[0, :, sl], cw_ref[:, sl]))
        if j < 2 * heads:
            y = y * lax.rsqrt(jnp.sum(y * y, axis=-1, keepdims=True) + NORM_EPS)
        if j < heads:
            q_ref[0, :, sl] = y * (DN_DK ** -0.5)
        elif j < 2 * heads:
            k_ref[0, :, slice((j - heads) * LANES, (j - heads + 1) * LANES)] = y
        else:
            v_ref[0, :, slice((j - 2 * heads) * LANES, (j - 2 * heads + 1) * LANES)] = y
    for j in range(xr_ref.shape[2] // LANES):
        sl = slice(j * LANES, (j + 1) * LANES)
        u_ref[0, :, sl] = conv(xr_ref[0, :, sl], xr_p_ref[0, :, sl], xr_n_ref[0, :, sl], xw_ref[:, sl]) + xb_ref[:, sl]
    ab = ab_ref[0]
    lane = lax.broadcasted_iota(jnp.int32, ab.shape, 1)
    g = nea_ref[...] * _softplus(ab + dtb_ref[...])
    gb_ref[0] = jnp.where(lane < 2 * heads, g, _sigmoid(ab))


def _even_prep(p, conv_qkv, conv_x_w, conv_x_b, neg_exp_a, dt_bias, r, n_lat, heads, width):
    b, s, _ = p.shape
    vw = heads * DN_DV
    qkvw = 3 * vw
    assert width == vw and qkvw % width == 0
    nb, nlb, r8 = s // r, n_lat // r, r // 8
    n8 = s // 8

    def cur(wd, cb):
        return pl.BlockSpec((1, r, wd), lambda b_, i: (b_, i, cb))

    def prev(wd, cb):
        return pl.BlockSpec((1, 8, wd), lambda b_, i: (b_, jnp.maximum(i * r8 - 1, 0), cb))

    def nxt(wd, cb):
        return pl.BlockSpec((1, 8, wd), lambda b_, i: (b_, jnp.minimum((i + 1) * r8, n8 - 1), cb))

    def par(shape):
        return pl.BlockSpec(shape, lambda b_, i: (0, 0))

    xcb = (qkvw + vw) // width
    abcb = (qkvw + vw + 2 * width) // LANES
    pad = LANES - 2 * heads
    nea = jnp.pad(neg_exp_a.reshape(1, 2 * heads), ((0, 0), (0, pad)))
    dtb = jnp.pad(dt_bias.reshape(1, 2 * heads), ((0, 0), (0, pad)))
    outs = [jax.ShapeDtypeStruct((b, s, vw), F32)] * 3 + [jax.ShapeDtypeStruct((b, s, width), F32),
                                                            jax.ShapeDtypeStruct((b, s, LANES), F32)]
    return pl.pallas_call(
        functools.partial(_even_prep_kernel, n_lat_blocks=nlb, n_blocks=nb, heads=heads, r=r),
        grid=(b, nb),
        in_specs=[cur(qkvw, 0), prev(qkvw, 0), nxt(qkvw, 0), cur(width, xcb), prev(width, xcb), nxt(width, xcb),
                  cur(LANES, abcb), par((4, qkvw)), par((4, width)), par((1, width)), par((1, LANES)), par((1, LANES))],
        out_specs=[cur(vw, 0), cur(vw, 0), cur(vw, 0), cur(width, 0), cur(LANES, 0)],
        out_shape=outs, compiler_params=_params("arbitrary", "arbitrary"), name="even_prep",
    )(p, p, p, p, p, p, p, conv_qkv, conv_x_w, conv_x_b.reshape(1, width), nea, dtb)


def _split2(x):
    hi = x.astype(BF16)
    return hi, (x - hi.astype(F32)).astype(BF16)


DN_CHUNKS_PER_STEP = 2
DN_PACK = 4


def _block_diag(x, row_blk, lane_blk, pack):
    tiled = jnp.concatenate([x] * pack, axis=0)
    return jnp.where(row_blk == lane_blk, tiled, 0.0).astype(BF16)


def _dot_hl(a, b_bd):
    n = a[0].shape[0]
    r = _dot(jnp.concatenate([a[0], a[1]], axis=0), b_bd)
    return r[:n] + r[n:]


def _delta_local_kernel(q_ref, k_ref, v_ref, g_ref, u_ref, wq_ref, ak_ref, gt_ref, *, heads, pack, cpb):
    c = DN_CHUNK
    wa = pack * c
    wk = pack * LANES
    ri = lax.broadcasted_iota(jnp.int32, (c, wa), 0)
    la = lax.broadcasted_iota(jnp.int32, (c, wa), 1)
    ci = la & (c - 1)
    blk_a = la >> 6
    blk_k = lax.broadcasted_iota(jnp.int32, (c, wk), 1) >> 7
    rb_a = lax.broadcasted_iota(jnp.int32, (wa, wa), 0) >> 6
    lb_a = lax.broadcasted_iota(jnp.int32, (wa, wa), 1) >> 6
    rb_k = lax.broadcasted_iota(jnp.int32, (wa, wk), 0) >> 6
    lb_k = lax.broadcasted_iota(jnp.int32, (wa, wk), 1) >> 7
    eye = jnp.where(ri == ci, 1.0, 0.0)
    ri1 = lax.broadcasted_iota(jnp.int32, (c, c), 0)
    ci1 = lax.broadcasted_iota(jnp.int32, (c, c), 1)

    def bd_a(x):
        return _block_diag(x, rb_a, lb_a, pack)

    def bd_k(x):
        return _block_diag(x, rb_k, lb_k, pack)

    def per_head(cols, blk):
        out = cols[0]
        for t in range(1, pack):
            out = jnp.where(blk == t, cols[t], out)
        return out

    incl = [(ri >= ci), (ri <= ci)]
    strict = [(ri > ci), (ri < ci)]
    last = [c - 1, 0]
    levels = [[strict[d] & ((ri >> (k + 1)) == (ci >> (k + 1))) & ((ri >> k) != (ci >> k)) for k in range(6)]
              for d in range(2)]
    ones = [jnp.where(ri1 >= ci1, 1.0, 0.0).astype(BF16), jnp.where(ri1 <= ci1, 1.0, 0.0).astype(BF16)]

    packs = []
    for j in range(cpb):
        rows = slice(j * c, (j + 1) * c)
        gall = g_ref[0, rows]
        g1 = gall.astype(BF16)
        rem = gall - g1.astype(F32)
        g2 = rem.astype(BF16)
        g3 = (rem - g2.astype(F32)).astype(BF16)
        for d in range(2):
            gcum = _dot(ones[d], g1) + (_dot(ones[d], g2) + _dot(ones[d], g3))
            gt_ref[d, j] = jnp.exp(gcum[last[d]:last[d] + 1, :])
            for h0 in range(0, heads, pack):
                cols = [d * heads + h0 + t for t in range(pack)]
                gcs = [gcum[:, cc:cc + 1] for cc in cols]
                gc_a = per_head([jnp.broadcast_to(x, (c, wa)) for x in gcs], blk_a)
                gr_a = jnp.sum(jnp.where(ri == ci, gc_a, 0.0), axis=0, keepdims=True)
                decay = jnp.where(incl[d], jnp.exp(jnp.where(incl[d], gc_a - gr_a, 0.0)), 0.0)
                beta_k = per_head([jnp.broadcast_to(gall[:, 2 * heads + cc:2 * heads + cc + 1], (c, wk))
                                   for cc in cols], blk_k)
                gc_k = per_head([jnp.broadcast_to(x, (c, wk)) for x in gcs], blk_k)
                gl_k = per_head([jnp.broadcast_to(x[last[d]:last[d] + 1], (c, wk)) for x in gcs], blk_k)
                sl = slice(h0 * LANES, (h0 + pack) * LANES)
                q_k, k_k, v_k = q_ref[0, rows, sl], k_ref[0, rows, sl], v_ref[0, rows, sl]
                kb_k = k_k * beta_k
                eg_k = jnp.exp(gc_k)
                packs.append(dict(j=j, d=d, h0=h0, decay=decay, kb=kb_k, q=q_k, k=k_k, eg=eg_k,
                                  rhs_u=v_k * beta_k, rhs_w=kb_k * eg_k, kd=k_k * jnp.exp(gl_k - gc_k)))

    for p in packs:
        lhs = jnp.concatenate([p["kb"], p["q"]], axis=0).astype(BF16)
        p["aa"] = _dot_nt(lhs, bd_k(p["k"]))
    for p in packs:
        d = p["d"]
        p["a"] = jnp.where(strict[d], p["aa"][:c] * p["decay"], 0.0).astype(BF16).astype(F32)
        p["a_qk"] = jnp.where(incl[d], p["aa"][c:] * p["decay"], 0.0)
        p["t"] = eye - jnp.where(levels[d][0], p["a"], 0.0)
    for k in range(1, 6):
        for p in packs:
            p["ts"] = _split2(p["t"])
            p["x"] = _dot_hl(p["ts"], bd_a(jnp.where(levels[p["d"]][k], p["a"], 0.0)))
        for p in packs:
            xs = _split2(p["x"])
            y = _dot_hl(xs, bd_a(p["ts"][0].astype(F32))) + _dot(xs[0], bd_a(p["ts"][1].astype(F32)))
            p["t"] = p["t"] - y
    for p in packs:
        p["ts"] = _split2(p["t"])
    for name in ("u", "w"):
        for p in packs:
            rh, rl = _split2(p["rhs_" + name])
            p[name] = _dot_hl(p["ts"], bd_k(rh.astype(F32))) + _dot(p["ts"][0], bd_k(rl.astype(F32)))
    for p in packs:
        j, d = p["j"], p["d"]
        qd_k = p["q"] * p["eg"]
        for t in range(pack):
            h = p["h0"] + t
            ks = slice(t * LANES, (t + 1) * LANES)
            u_ref[d, j, h] = p["u"][:, ks]
            wq_ref[d, j, h, 0:c] = p["w"][:, ks].astype(BF16)
            wq_ref[d, j, h, c:2 * c] = qd_k[:, ks].astype(BF16)
            ak_ref[d, j, h, 0:c] = p["a_qk"][:, t * c:(t + 1) * c].astype(BF16)
            ak_ref[d, j, h, c:c + DN_DK] = p["kd"][:, ks].T.astype(BF16)


def _delta_scan_kernel(uf, wqf, akf, gtf, ub, wqb, akb, gtb, of, ob, state, *, heads):
    c = DN_CHUNK

    @pl.when(pl.program_id(1) == 0)
    def _():
        state[...] = jnp.zeros_like(state)

    dirs = ((uf, wqf, akf, gtf, of), (ub, wqb, akb, gtb, ob))
    chains = [(d, h) for d in range(2) for h in range(heads)]
    s_prev = {ch: state[ch[0], ch[1]] for ch in chains}
    r1 = {(d, h): _dot(dirs[d][1][h], s_prev[(d, h)].astype(BF16)) for d, h in chains}
    r2 = {}
    for d, h in chains:
        v_new = dirs[d][0][h] - r1[(d, h)][:c]
        r2[(d, h)] = _dot(dirs[d][2][h], v_new.astype(BF16))
    for d, h in chains:
        gt = dirs[d][3][...]
        dirs[d][4][0, :, h * LANES:(h + 1) * LANES] = r1[(d, h)][c:] + r2[(d, h)][:c]
        state[d, h] = s_prev[(d, h)] * gt[:, d * heads + h:d * heads + h + 1] + r2[(d, h)][c:]


def _delta(q, k, v, gates, n_lat, heads):
    b, s, vw = q.shape
    c = DN_CHUNK
    n_chunks, n_lat_c = s // c, n_lat // c
    n_ctx_c = n_chunks - n_lat_c
    cpb = _pick(n_chunks, (DN_CHUNKS_PER_STEP, 1))
    seq = pl.BlockSpec((1, cpb * c, vw), lambda b_, i: (b_, i, 0))

    def loc(rows, cols):
        return pl.BlockSpec((2, None, cpb, heads, rows, cols), lambda b_, i: (0, b_, i, 0, 0, 0))

    u, wq, ak, gt = pl.pallas_call(
        functools.partial(_delta_local_kernel, heads=heads, pack=math.gcd(heads, DN_PACK), cpb=cpb),
        grid=(b, n_chunks // cpb),
        in_specs=[seq, seq, seq, pl.BlockSpec((1, cpb * c, LANES), lambda b_, i: (b_, i, 0))],
        out_specs=[loc(c, DN_DV), loc(2 * c, DN_DV), loc(c + DN_DK, c),
                   pl.BlockSpec((2, None, cpb, 1, LANES), lambda b_, i: (0, b_, i, 0, 0))],
        out_shape=[jax.ShapeDtypeStruct((2, b, n_chunks, heads, c, DN_DV), F32),
                   jax.ShapeDtypeStruct((2, b, n_chunks, heads, 2 * c, DN_DV), BF16),
                   jax.ShapeDtypeStruct((2, b, n_chunks, heads, c + DN_DK, c), BF16),
                   jax.ShapeDtypeStruct((2, b, n_chunks, 1, LANES), F32)],
        compiler_params=_params("arbitrary", "arbitrary"), name="delta_local",
    )(q, k, v, gates)

    def fwd(st):
        return jnp.where(st < n_ctx_c, n_lat_c + st, st - n_ctx_c)

    def bwd(st):
        return jnp.where(st < n_ctx_c, n_chunks - 1 - st, n_lat_c - 1 - (st - n_ctx_c))

    def chunk(d, order, rows, cols):
        return pl.BlockSpec((None, None, None, heads, rows, cols), lambda b_, st: (d, b_, order(st), 0, 0, 0))

    def gspec(d, order):
        return pl.BlockSpec((None, None, None, 1, LANES), lambda b_, st: (d, b_, order(st), 0, 0))

    def ospec(order):
        return pl.BlockSpec((1, c, vw), lambda b_, st: (b_, order(st), 0))

    ins, args = [], []
    for d, order in ((0, fwd), (1, bwd)):
        ins += [chunk(d, order, c, DN_DV), chunk(d, order, 2 * c, DN_DV), chunk(d, order, c + DN_DK, c),
                gspec(d, order)]
        args += [u, wq, ak, gt]
    out = jax.ShapeDtypeStruct((b, s, vw), F32)
    return pl.pallas_call(
        functools.partial(_delta_scan_kernel, heads=heads), grid=(b, n_chunks),
        in_specs=ins, out_specs=[ospec(fwd), ospec(bwd)], out_shape=[out, out],
        scratch_shapes=[pltpu.VMEM((2, heads, DN_DK, DN_DV), F32)],
        compiler_params=_params("arbitrary", "arbitrary"), name="delta_scan",
    )(*args)


def _lru_kernel(uf_ref, ub_ref, wri_ref, bri_ref, spl_ref, hf_ref, hb_ref, carry, *, r, n_blk):
    @pl.when(pl.program_id(1) == 0)
    def _():
        carry[...] = jnp.zeros_like(carry)

    row = lax.broadcasted_iota(jnp.int32, (r, 1), 0)
    for d, (u_ref, h_ref) in enumerate(((uf_ref, hf_ref), (ub_ref, hb_ref))):
        for n in range(n_blk):
            sl = slice(n * LANES, (n + 1) * LANES)
            u = u_ref[0, :, sl]
            ri = _dot(u.astype(BF16), wri_ref[d, n]) + bri_ref[d, n]
            rg = _sigmoid(ri[:, :LANES])
            ig = _sigmoid(ri[:, LANES:])
            log_a = -spl_ref[d, n] * rg
            a = jnp.exp(log_a)
            bt = jnp.sqrt(-_expm1_nonpos(2.0 * log_a)) * (ig * u)
            sh = 1
            while sh < r:
                if d == 0:
                    keep = row >= sh
                    a_s, b_s = pltpu.roll(a, sh, 0), pltpu.roll(bt, sh, 0)
                else:
                    keep = row < r - sh
                    a_s, b_s = pltpu.roll(a, r - sh, 0), pltpu.roll(bt, r - sh, 0)
                bt = a * jnp.where(keep, b_s, 0.0) + bt
                a = a * jnp.where(keep, a_s, 1.0)
                sh *= 2
            h = bt + a * carry[d, :, sl]
            h_ref[0, :, sl] = h
            carry[d, :, sl] = h[r - 1:r] if d == 0 else h[0:1]


def _lru(u, w_ri, b_ri, spl, r, n_lat):
    b, s, width = u.shape
    n_blk = width // LANES
    nb, nlb = s // r, n_lat // r
    ncb = nb - nlb

    def fwd(st):
        return jnp.where(st < ncb, nlb + st, st - ncb)

    def bwd(st):
        return jnp.where(st < ncb, nb - 1 - st, nlb - 1 - (st - ncb))

    def seq(order):
        return pl.BlockSpec((1, r, width), lambda b_, st: (b_, order(st), 0))

    def par(shape):
        return pl.BlockSpec(shape, lambda b_, st: (0,) * len(shape))

    out = jax.ShapeDtypeStruct((b, s, width), F32)
    return pl.pallas_call(
        functools.partial(_lru_kernel, r=r, n_blk=n_blk), grid=(b, nb),
        in_specs=[seq(fwd), seq(bwd), par(w_ri.shape), par(b_ri.shape), par(spl.shape)],
        out_specs=[seq(fwd), seq(bwd)], out_shape=[out, out],
        scratch_shapes=[pltpu.VMEM((2, 1, width), F32)],
        compiler_params=_params("arbitrary", "arbitrary"), name="lru",
    )(u, u, w_ri, b_ri, spl)


def _gelu_tanh(x):
    return 0.5 * x * (1.0 + jnp.tanh(math.sqrt(2.0 / math.pi) * (x + 0.044715 * (x * x * x))))


def _even_out_kernel(of_ref, ob_ref, z_ref, hf_ref, hb_ref, y_ref, on_ref, mix_ref, *, heads, n_blk):
    for h in range(heads):
        sl = slice(h * LANES, (h + 1) * LANES)
        o = of_ref[0, :, sl] + ob_ref[0, :, sl]
        o = o * lax.rsqrt(jnp.mean(o * o, axis=-1, keepdims=True) + NORM_EPS) * on_ref[...]
        mix_ref[0, :, sl] = (o * _silu(z_ref[0, :, sl])).astype(BF16)
    for n in range(n_blk):
        sl = slice(n * LANES, (n + 1) * LANES)
        osl = slice((heads + n) * LANES, (heads + n + 1) * LANES)
        mix_ref[0, :, osl] = ((hf_ref[0, :, sl] + hb_ref[0, :, sl]) * _gelu_tanh(y_ref[0, :, sl])).astype(BF16)


def _even_out(o_f, o_b, p, h_f, h_b, o_norm, r, heads):
    b, s, vw = o_f.shape
    width = h_f.shape[2]

    def cur(wd, cb):
        return pl.BlockSpec((1, r, wd), lambda b_, i: (b_, i, cb))

    zcb = (3 * vw) // vw
    ycb = (4 * vw + width) // width
    return pl.pallas_call(
        functools.partial(_even_out_kernel, heads=heads, n_blk=width // LANES), grid=(b, s // r),
        in_specs=[cur(vw, 0), cur(vw, 0), cur(vw, zcb), cur(width, 0), cur(width, 0), cur(width, ycb),
                  pl.BlockSpec((1, DN_DV), lambda b_, i: (0, 0))],
        out_specs=cur(vw + width, 0), out_shape=jax.ShapeDtypeStruct((b, s, vw + width), BF16),
        compiler_params=_params("arbitrary", "arbitrary"), name="even_out",
    )(o_f, o_b, p, h_f, h_b, p, o_norm.reshape(1, DN_DV))


def _rope_pair(y, tab):
    y = y * tab
    return y + pltpu.roll(y, MLA_ROPE, 1)


def _q_proj_kernel(a_ref, w_ref, tab_ref, q_ref, *, scale):
    acc = _dot(a_ref[0], w_ref[0])
    q_ref[0, 0, :, 0:MLA_NOPE] = (acc[:, :MLA_NOPE] * scale).astype(BF16)
    qr = _rope_pair(acc[:, MLA_NOPE:], tab_ref[...])
    q_ref[0, 0, :, MLA_NOPE:MLA_QD] = (qr[:, :MLA_ROPE] * scale).astype(BF16)


def _q_proj(h, w_q, tab, heads):
    b, s, d = h.shape
    tm = _pick(s, (768, 512, 640, 256, 128))
    return pl.pallas_call(
        functools.partial(_q_proj_kernel, scale=math.log2(math.e) * MLA_QD ** -0.5), grid=(b, s // tm, heads),
        in_specs=[pl.BlockSpec((1, tm, d), lambda b_, i, hd: (b_, i, 0)),
                  pl.BlockSpec((1, d, 2 * LANES), lambda b_, i, hd: (hd, 0, 0)),
                  pl.BlockSpec((tm, LANES), lambda b_, i, hd: (i, 0))],
        out_specs=pl.BlockSpec((1, 1, tm, MLA_QD), lambda b_, i, hd: (b_, hd, i, 0)),
        out_shape=jax.ShapeDtypeStruct((b, heads, s, MLA_QD), BF16),
        compiler_params=_params("arbitrary", "arbitrary", "arbitrary"), name="q_proj",
    )(h, w_q, tab)


def _ckv_proj_kernel(a_ref, w_ref, tab_ref, g_ref, ckv_ref, kr_ref, *, rank):
    acc = _dot(a_ref[0], w_ref[...])
    ckv = acc[:, :rank]
    ckv = ckv * lax.rsqrt(jnp.mean(ckv * ckv, axis=-1, keepdims=True) + NORM_EPS) * g_ref[...]
    ckv_ref[0] = ckv.astype(BF16)
    kr_ref[0] = _rope_pair(acc[:, rank:], tab_ref[...]).astype(BF16)


def _ckv_proj(h, w_c, tab, kv_norm):
    b, s, d = h.shape
    rank = kv_norm.shape[0]
    tm = _pick(s, (768, 512, 640, 256, 128))
    return pl.pallas_call(
        functools.partial(_ckv_proj_kernel, rank=rank), grid=(b, s // tm),
        in_specs=[pl.BlockSpec((1, tm, d), lambda b_, i: (b_, i, 0)),
                  pl.BlockSpec((d, rank + LANES), lambda b_, i: (0, 0)),
                  pl.BlockSpec((tm, LANES), lambda b_, i: (i, 0)),
                  pl.BlockSpec((1, rank), lambda b_, i: (0, 0))],
        out_specs=[pl.BlockSpec((1, tm, rank), lambda b_, i: (b_, i, 0)),
                   pl.BlockSpec((1, tm, LANES), lambda b_, i: (b_, i, 0))],
        out_shape=[jax.ShapeDtypeStruct((b, s, rank), BF16), jax.ShapeDtypeStruct((b, s, LANES), BF16)],
        compiler_params=_params("arbitrary", "arbitrary"), name="ckv_proj",
    )(h, w_c, tab, kv_norm.reshape(1, rank))


def _kv_up_kernel(a_ref, w_ref, kr_ref, k_ref, v_ref):
    acc = _dot(a_ref[0], w_ref[0])
    k_ref[0, 0, :, 0:MLA_NOPE] = acc[:, :MLA_NOPE].astype(BF16)
    k_ref[0, 0, :, MLA_NOPE:MLA_QD] = kr_ref[0, :, 0:MLA_ROPE]
    v_ref[0, 0] = acc[:, MLA_NOPE:].astype(BF16)


def _kv_up(ckv, w_ukv, kr, heads):
    b, s, rank = ckv.shape
    tm = _pick(s, (768, 512, 640, 256, 128))
    return pl.pallas_call(
        _kv_up_kernel, grid=(b, s // tm, heads),
        in_specs=[pl.BlockSpec((1, tm, rank), lambda b_, i, hd: (b_, i, 0)),
                  pl.BlockSpec((1, rank, MLA_NOPE + MLA_V), lambda b_, i, hd: (hd, 0, 0)),
                  pl.BlockSpec((1, tm, LANES), lambda b_, i, hd: (b_, i, 0))],
        out_specs=[pl.BlockSpec((1, 1, tm, MLA_QD), lambda b_, i, hd: (b_, hd, i, 0)),
                   pl.BlockSpec((1, 1, tm, MLA_V), lambda b_, i, hd: (b_, hd, i, 0))],
        out_shape=[jax.ShapeDtypeStruct((b, heads, s, MLA_QD), BF16),
                   jax.ShapeDtypeStruct((b, heads, s, MLA_V), BF16)],
        compiler_params=_params("arbitrary", "arbitrary", "arbitrary"), name="kv_up",
    )(ckv, w_ukv, kr)


def _flash_kernel(q_ref, k_ref, v_ref, o_ref, m_ref, l_ref, acc_ref, *, sub):
    j = pl.program_id(3)

    @pl.when(j == 0)
    def _():
        m_ref[...] = jnp.full_like(m_ref, -jnp.inf)
        l_ref[...] = jnp.zeros_like(l_ref)
        acc_ref[...] = jnp.zeros_like(acc_ref)

    k = k_ref[0, 0]
    v = v_ref[0, 0]
    for c in range(q_ref.shape[2] // sub):
        rows = slice(c * sub, (c + 1) * sub)
        s = _dot_nt(q_ref[0, 0, rows, :], k)
        m_prev = m_ref[rows]
        m_new = jnp.maximum(m_prev, jnp.max(s, axis=-1, keepdims=True))
        p = jnp.exp2(s - m_new)
        alpha = jnp.exp2(m_prev - m_new)
        l_ref[rows] = alpha * l_ref[rows] + jnp.sum(p, axis=-1, keepdims=True)
        acc_ref[rows] = alpha * acc_ref[rows] + _dot(p.astype(BF16), v)
        m_ref[rows] = m_new

    @pl.when(j == pl.num_programs(3) - 1)
    def _():
        o_ref[0] = (acc_ref[...] / l_ref[...]).astype(o_ref.dtype)


def _attn_full_kernel(q_ref, k_ref, v_ref, o_ref, *, sub):
    k = k_ref[0, 0]
    v = v_ref[0, 0]
    for c in range(q_ref.shape[2] // sub):
        rows = slice(c * sub, (c + 1) * sub)
        s = _dot_nt(q_ref[0, 0, rows, :], k)
        p = jnp.exp2(s - jnp.max(s, axis=-1, keepdims=True))
        l = jnp.sum(p, axis=-1, keepdims=True)
        o_ref[0, rows, :] = (_dot(p.astype(BF16), v) / l).astype(o_ref.dtype)


def _flash(q, k, v, q_rows, kv_rows):
    b, heads, _, dq = q.shape
    dv = v.shape[3]
    q0, nq = q_rows
    k0, nk = kv_rows
    tq = _pick(math.gcd(nq, q0) if q0 else nq, (2048, 1024, 512, 256, 128))
    tk = _pick(math.gcd(nk, k0) if k0 else nk, (nk, 768, 512, 256, 128))
    qo, ko = q0 // tq, k0 // tk
    if nk == tk:
        return pl.pallas_call(
            functools.partial(_attn_full_kernel, sub=min(tq, FLASH_SUB)), grid=(b, heads, nq // tq),
            in_specs=[pl.BlockSpec((1, 1, tq, dq), lambda b_, h, i: (b_, h, qo + i, 0)),
                      pl.BlockSpec((1, 1, tk, dq), lambda b_, h, i: (b_, h, ko, 0)),
                      pl.BlockSpec((1, 1, tk, dv), lambda b_, h, i: (b_, h, ko, 0))],
            out_specs=pl.BlockSpec((1, tq, dv), lambda b_, h, i: (b_, i, h)),
            out_shape=jax.ShapeDtypeStruct((b, nq, heads * dv), BF16),
            compiler_params=_params("arbitrary", "arbitrary", "arbitrary"), name="attn_full",
        )(q, k, v)
    return pl.pallas_call(
        functools.partial(_flash_kernel, sub=min(tq, FLASH_SUB)), grid=(b, heads, nq // tq, nk // tk),
        in_specs=[pl.BlockSpec((1, 1, tq, dq), lambda b_, h, i, j: (b_, h, qo + i, 0)),
                  pl.BlockSpec((1, 1, tk, dq), lambda b_, h, i, j: (b_, h, ko + j, 0)),
                  pl.BlockSpec((1, 1, tk, dv), lambda b_, h, i, j: (b_, h, ko + j, 0))],
        out_specs=pl.BlockSpec((1, tq, dv), lambda b_, h, i, j: (b_, i, h)),
        out_shape=jax.ShapeDtypeStruct((b, nq, heads * dv), BF16),
        scratch_shapes=[pltpu.VMEM((tq, 1), F32), pltpu.VMEM((tq, 1), F32), pltpu.VMEM((tq, dv), F32)],
        compiler_params=_params("arbitrary", "arbitrary", "arbitrary", "arbitrary"), name="flash",
    )(q, k, v)


def _experts_kernel(be_ref, nu_ref, x_ref, w1_ref, w3_ref, w2_ref, y_ref):
    @pl.when(pl.program_id(0) < nu_ref[0])
    def _():
        x = x_ref[...].astype(BF16)
        h1 = _dot(x, w1_ref[0].astype(BF16))
        h3 = _dot(x, w3_ref[0].astype(BF16))
        hh = (_silu(h1) * h3).astype(BF16)
        y_ref[...] = _dot(hh, w2_ref[0].astype(BF16)).astype(y_ref.dtype)


def _experts(xg, blk_e, n_used, w1, w3, w2, layer, tm):
    rows, d = xg.shape
    de = w1.shape[3]
    def used(i, nu):
        return jnp.minimum(i, nu[0] - 1)

    grid_spec = pltpu.PrefetchScalarGridSpec(
        num_scalar_prefetch=2, grid=(rows // tm,),
        in_specs=[pl.BlockSpec((tm, d), lambda i, be, nu: (used(i, nu), 0)),
                  pl.BlockSpec((None, 1, d, de), lambda i, be, nu: (layer, be[used(i, nu)], 0, 0)),
                  pl.BlockSpec((None, 1, d, de), lambda i, be, nu: (layer, be[used(i, nu)], 0, 0)),
                  pl.BlockSpec((None, 1, de, d), lambda i, be, nu: (layer, be[used(i, nu)], 0, 0))],
        out_specs=pl.BlockSpec((tm, d), lambda i, be, nu: (used(i, nu), 0)))
    return pl.pallas_call(
        _experts_kernel, grid_spec=grid_spec, out_shape=jax.ShapeDtypeStruct((rows, d), BF16),
        compiler_params=_params("arbitrary"), name="experts",
    )(blk_e, n_used, xg, w1, w3, w2)


def _moe(h, logits, n_groups, n_experts, w1, w3, w2, layer, tm):
    n_tok, d = h.shape
    epg = n_experts // n_groups
    lg = logits[:, :n_groups]
    grp = jnp.argmax(lg, axis=-1)
    p_grp = jnp.take_along_axis(jax.nn.softmax(lg, -1), grp[:, None], -1)
    le = logits[:, n_groups:n_groups + n_experts].reshape(n_tok, n_groups, epg)
    le = jnp.take_along_axis(le, grp[:, None, None], axis=1)[:, 0]
    top_v, top_i = lax.top_k(le, TOP_K)
    gate = p_grp * jax.nn.softmax(top_v, -1)
    expert = (grp[:, None] * epg + top_i).reshape(-1).astype(jnp.int32)
    onehot = (expert[:, None] == jnp.arange(n_experts, dtype=jnp.int32)[None, :]).astype(jnp.int32)
    rank = jnp.sum((jnp.cumsum(onehot, axis=0) - onehot) * onehot, axis=1)
    counts = jnp.sum(onehot, axis=0)
    padded = (counts + tm - 1) // tm * tm
    pend = jnp.cumsum(padded)
    dest = (pend - padded)[expert] + rank
    n_blk = (n_tok * TOP_K + tm - 1) // tm + n_experts
    blk_e = jnp.minimum(jnp.sum(pend[None, :] <= (jnp.arange(n_blk) * tm)[:, None], axis=1), n_experts - 1)
    n_used = (pend[-1] // tm).reshape(1)
    src = (jnp.arange(n_blk * tm, dtype=jnp.int32) % n_tok).at[dest].set(
        jnp.arange(n_tok * TOP_K, dtype=jnp.int32) // TOP_K)
    y = _experts(h[src], blk_e.astype(jnp.int32), n_used.astype(jnp.int32), w1, w3, w2, layer, tm)
    dest = dest.reshape(n_tok, TOP_K)
    return (y[dest[:, 0]], y[dest[:, 1]]), (gate[:, 0:1], gate[:, 1:2])


def _rope_table(n_lat, n_ctx):
    rows = n_lat // GRID_W
    row = jnp.broadcast_to(jnp.arange(rows)[:, None], (rows, GRID_W)).reshape(-1)
    col = jnp.broadcast_to(jnp.arange(GRID_W)[None, :], (rows, GRID_W)).reshape(-1)
    pos = jnp.stack([row, col], -1).astype(F32)
    inv = ROPE_BASE ** (-jnp.arange(ROPE_FREQS, dtype=F32) / ROPE_FREQS)
    ang = pos[:, :, None] * inv
    ang = jnp.broadcast_to(ang[:, :, None, :], (n_lat, 2, 2, ROPE_FREQS)).reshape(n_lat, MLA_ROPE)
    lat = jnp.concatenate([jnp.cos(ang), jnp.sin(ang)], axis=-1)
    ctx = jnp.concatenate([jnp.ones((n_ctx, MLA_ROPE), F32), jnp.zeros((n_ctx, MLA_ROPE), F32)], axis=-1)
    return jnp.concatenate([lat, ctx], axis=0)


def _rot_cols(w):
    wr = w.reshape(w.shape[:-1] + (2, 2, ROPE_FREQS))
    return jnp.stack([-wr[..., 1, :], wr[..., 0, :]], axis=-2).reshape(w.shape)


def kernel(x, c, ctx, c_ctx, ada_w, ada_b, ln_mix_g, ln_mix_b, ln_ffn_g, ln_ffn_b, ev_w_in, ev_conv_qkv, ev_a_log,
           ev_dt_bias, ev_o_norm, ev_conv_x_w, ev_conv_x_b, ev_w_r, ev_b_r, ev_w_i, ev_b_i, ev_lam, ev_w_out,
           od_w_in, od_kv_norm, od_w_ukv, od_w_out, moe_w_grp, moe_b_grp, moe_w_exp, moe_b_exp, moe_w1, moe_w3,
           moe_w2):
    bsz, n_lat, d = x.shape
    n_ctx = ctx.shape[1]
    s = n_lat + n_ctx
    depth = ada_w.shape[0]
    alpha = (2.0 * depth) ** 0.25
    r = min(256, n_ctx)
    assert n_lat % r == 0 and n_ctx % r == 0 and n_lat % GRID_W == 0 and bsz + 1 <= 8
    dn_heads = ev_a_log.shape[-1]
    vw = dn_heads * DN_DV
    width = ev_lam.shape[-1]
    lru_blocks = ev_w_r.shape[2]
    assert width // lru_blocks == LANES
    rank = od_kv_norm.shape[-1]
    mla_heads = od_w_ukv.shape[-1] // (MLA_NOPE + MLA_V)
    n_groups, n_experts = moe_w_grp.shape[-1], moe_w_exp.shape[-1]
    moe_tm = 256 if (bsz * s * TOP_K) // n_experts >= 512 else 128

    xs = jnp.concatenate([x, ctx], axis=1)
    c_all = jnp.zeros((8, d), F32).at[:bsz].set(c).at[bsz].set(c_ctx)
    mods = _adaln(c_all, ada_w, ada_b).reshape(depth, 8, 6, d)

    def seg_tab(layer, k):
        lat = mods[layer, :bsz, k]
        ctx_v = jnp.broadcast_to(mods[layer, bsz, k][None], (bsz, d))
        return jnp.stack([lat, ctx_v], axis=1)[:, :, None, :]

    rope_tab = _rope_table(n_lat, n_ctx)
    hmod = _modulate(xs, seg_tab(0, 0), seg_tab(0, 1), r, n_lat)
    for layer in range(depth):
        i = layer // 2
        last = layer == depth - 1
        if layer % 2 == 0:
            qkvw = 3 * vw
            w = ev_w_in[i]
            o_z, o_a, o_b, o_x, o_y = qkvw, qkvw + vw, qkvw + vw + 2 * dn_heads, qkvw + vw + 4 * dn_heads, \
                qkvw + vw + 4 * dn_heads + width
            w_perm = jnp.concatenate([w[:, :o_a], w[:, o_x:], w[:, o_a:o_x],
                                      jnp.zeros((d, LANES - 4 * dn_heads), F32)], axis=1).astype(BF16)
            p = _mm(hmod.reshape(bsz * s, d), w_perm).reshape(bsz, s, -1)
            q, k, v, u, gb = _even_prep(p, ev_conv_qkv[i], ev_conv_x_w[i], ev_conv_x_b[i], -jnp.exp(ev_a_log[i]),
                                        ev_dt_bias[i], r, n_lat, dn_heads, width)
            o_f, o_b = _delta(q, k, v, gb, n_lat, dn_heads)
            w_ri = jnp.concatenate([ev_w_r[i], ev_w_i[i]], axis=-1).astype(BF16)
            b_ri = jnp.concatenate([ev_b_r[i].reshape(2, lru_blocks, 1, LANES),
                                    ev_b_i[i].reshape(2, lru_blocks, 1, LANES)], axis=-1)
            spl = (LRU_C * jax.nn.softplus(-ev_lam[i])).reshape(2, lru_blocks, 1, LANES)
            h_f, h_b = _lru(u, w_ri, b_ri, spl, r, n_lat)
            mix = _even_out(o_f, o_b, p, h_f, h_b, ev_o_norm[i], r, dn_heads)
            w_out = ev_w_out[i]
        else:
            w = od_w_in[i]
            nq = mla_heads * MLA_QD
            wq = w[:, :nq].reshape(d, mla_heads, MLA_QD)
            wq = jnp.concatenate([wq, _rot_cols(wq[..., MLA_NOPE:])], axis=-1).transpose(1, 0, 2).astype(BF16)
            w_kr = w[:, nq + rank:]
            w_c = jnp.concatenate([w[:, nq:nq + rank], w_kr, _rot_cols(w_kr)], axis=-1).astype(BF16)
            w_u = od_w_ukv[i].reshape(rank, mla_heads, MLA_NOPE + MLA_V).transpose(1, 0, 2).astype(BF16)
            qh = _q_proj(hmod, wq, rope_tab, mla_heads)
            ckv, kr = _ckv_proj(hmod, w_c, rope_tab, od_kv_norm[i])
            kh, vh = _kv_up(ckv, w_u, kr, mla_heads)
            mix = jnp.concatenate([_flash(qh, kh, vh, (0, n_lat), (0, s)),
                                   _flash(qh, kh, vh, (n_lat, n_ctx), (n_lat, n_ctx))], axis=1)
            w_out = od_w_out[i]
        y = _mm(mix.reshape(bsz * s, -1), w_out.astype(BF16), BF16).reshape(bsz, s, d)
        nr = LANES * ((n_groups + n_experts + LANES - 1) // LANES)
        w_rt = jnp.pad(jnp.concatenate([moe_w_grp[layer], moe_w_exp[layer]], axis=1),
                       ((0, 0), (0, nr - n_groups - n_experts)))
        b_rt = jnp.pad(jnp.concatenate([moe_b_grp[layer], moe_b_exp[layer]]), (0, nr - n_groups - n_experts))
        xs, hf, logits = _postnorm(xs, [y], seg_tab(layer, 2), ln_mix_g[layer], ln_mix_b[layer], r, n_lat, alpha,
                                   shift=seg_tab(layer, 3), scale=seg_tab(layer, 4), router=(w_rt, b_rt[None]))
        ys, gts = _moe(hf.reshape(bsz * s, d), logits.reshape(bsz * s, nr), n_groups, n_experts,
                       moe_w1, moe_w3, moe_w2, layer, moe_tm)
        ys = [t.reshape(bsz, s, d) for t in ys]
        gts = [t.reshape(bsz, s, 1) for t in gts]
        if last:
            (xs,) = _postnorm(xs, ys, seg_tab(layer, 5), ln_ffn_g[layer], ln_ffn_b[layer], r, n_lat, alpha,
                              rows=n_lat, row_gates=gts)
        else:
            xs, hmod = _postnorm(xs, ys, seg_tab(layer, 5), ln_ffn_g[layer], ln_ffn_b[layer], r, n_lat, alpha,
                                 shift=seg_tab(layer + 1, 0), scale=seg_tab(layer + 1, 1), row_gates=gts)
    return xs
```

```python
import functools
import math

import jax
import jax.numpy as jnp
from jax import lax
from jax.experimental import pallas as pl
from jax.experimental.pallas import tpu as pltpu

F32 = jnp.float32
BF16 = jnp.bfloat16
HIGHEST = lax.Precision.HIGHEST

LANES = 128
DN_DK = 128
DN_DV = 128
DN_CHUNK = 64
MLA_NOPE = 128
MLA_ROPE = 64
MLA_V = 128
MLA_QD = MLA_NOPE + MLA_ROPE
GRID_W = 64
ROPE_FREQS = MLA_ROPE // 4
ROPE_BASE = 10000.0
LRU_C = 8.0
LN_EPS = 1e-5
NORM_EPS = 1e-6
TOP_K = 2
FLASH_SUB = 256
VMEM_LIMIT = 48 * 1024 * 1024


def _pick(n, cands):
    for c in cands:
        if n % c == 0:
            return c
    raise ValueError(f"no tile for {n} in {cands}")


def _params(*sem):
    return pltpu.CompilerParams(dimension_semantics=sem, vmem_limit_bytes=VMEM_LIMIT)


def _dot(a, b, precision=None):
    return jnp.dot(a, b, preferred_element_type=F32, precision=precision)


def _dot_nt(a, b, precision=None):
    return lax.dot_general(a, b, (((1,), (1,)), ((), ())), preferred_element_type=F32, precision=precision)


def _sigmoid(x):
    return 1.0 / (1.0 + jnp.exp(-x))


def _silu(x):
    return x * _sigmoid(x)


def _softplus(x):
    return jnp.maximum(x, 0.0) + jnp.log1p(jnp.exp(-jnp.abs(x)))


def _expm1_nonpos(x):
    u = jnp.exp(x)
    safe = (x > -0.5) & (u < 1.0)
    stable = (u - 1.0) * x / jnp.log(jnp.where(safe, u, 0.5))
    return jnp.where(safe, stable, jnp.where(u < 1.0, u - 1.0, x))


def _adaln_kernel(c_ref, w_ref, b_ref, o_ref):
    a = _silu(c_ref[...])
    o_ref[0] = _dot(a.astype(BF16), w_ref[0].astype(BF16)) + b_ref[0]


def _adaln(c_all, ada_w, ada_b):
    n_layer, d, n6 = ada_w.shape
    tn = _pick(n6, (1024, 512, 256, 128))
    return pl.pallas_call(
        _adaln_kernel, grid=(n_layer, n6 // tn),
        in_specs=[pl.BlockSpec((8, d), lambda l, j: (0, 0)),
                  pl.BlockSpec((1, d, tn), lambda l, j: (l, 0, j)),
                  pl.BlockSpec((1, 1, tn), lambda l, j: (l, 0, j))],
        out_specs=pl.BlockSpec((1, 8, tn), lambda l, j: (l, 0, j)),
        out_shape=jax.ShapeDtypeStruct((n_layer, 8, n6), F32),
        compiler_params=_params("arbitrary", "arbitrary"), name="adaln",
    )(c_all, ada_w, ada_b.reshape(n_layer, 1, n6))


def _seg_spec(d, n_lat_blocks):
    return pl.BlockSpec((1, 1, 1, d), lambda b, i: (b, jnp.where(i >= n_lat_blocks, 1, 0), 0, 0))


def _modulate_kernel(x_ref, sh_ref, sc_ref, h_ref):
    h_ref[0] = (x_ref[0] * (1.0 + sc_ref[0, 0]) + sh_ref[0, 0]).astype(BF16)


def _modulate(x, shift, scale, r, n_lat):
    b, s, d = x.shape
    row = pl.BlockSpec((1, r, d), lambda b_, i: (b_, i, 0))
    return pl.pallas_call(
        _modulate_kernel, grid=(b, s // r),
        in_specs=[row, _seg_spec(d, n_lat // r), _seg_spec(d, n_lat // r)],
        out_specs=row, out_shape=jax.ShapeDtypeStruct((b, s, d), BF16),
        compiler_params=_params("arbitrary", "arbitrary"), name="modulate",
    )(x, shift, scale)


def _postnorm_kernel(*refs, alpha, n_y, row_gated, with_h, with_router):
    x_ref = refs[0]
    pos = 1
    y = None
    for _ in range(n_y):
        term = refs[pos][0].astype(F32)
        pos += 1
        if row_gated:
            term = term * refs[pos][0]
            pos += 1
        y = term if y is None else y + term
    gate_ref, g_ref, b_ref = refs[pos:pos + 3]
    pos += 3
    if with_h:
        sh_ref, sc_ref = refs[pos:pos + 2]
        pos += 2
    if with_router:
        wr_ref, br_ref = refs[pos:pos + 2]
        pos += 2
    xo_ref = refs[pos]
    v = alpha * x_ref[0] + gate_ref[0, 0] * y
    mu = jnp.mean(v, axis=-1, keepdims=True)
    vc = v - mu
    var = jnp.mean(vc * vc, axis=-1, keepdims=True)
    xn = vc * lax.rsqrt(var + LN_EPS) * g_ref[...] + b_ref[...]
    xo_ref[0] = xn
    if with_h:
        h = xn * (1.0 + sc_ref[0, 0]) + sh_ref[0, 0]
        refs[pos + 1][0] = h.astype(BF16)
        if with_router:
            refs[pos + 2][0] = _dot(h, wr_ref[...], HIGHEST) + br_ref[...]


def _postnorm(x, ys, gate, ln_g, ln_b, r, n_lat, alpha, shift=None, scale=None, router=None, rows=None,
              row_gates=None):
    b, s, d = x.shape
    rows = s if rows is None else rows
    nlb = n_lat // r
    row = pl.BlockSpec((1, r, d), lambda b_, i: (b_, i, 0))
    col1 = pl.BlockSpec((1, r, 1), lambda b_, i: (b_, i, 0))
    vec = pl.BlockSpec((1, d), lambda b_, i: (0, 0))
    args, in_specs = [x], [row]
    for j, y in enumerate(ys):
        args.append(y)
        in_specs.append(row)
        if row_gates is not None:
            args.append(row_gates[j])
            in_specs.append(col1)
    args += [gate, ln_g.reshape(1, d), ln_b.reshape(1, d)]
    in_specs += [_seg_spec(d, nlb), vec, vec]
    out_shape = [jax.ShapeDtypeStruct((b, rows, d), F32)]
    out_specs = [row]
    with_h = shift is not None
    if with_h:
        args += [shift, scale]
        in_specs += [_seg_spec(d, nlb), _seg_spec(d, nlb)]
        out_shape.append(jax.ShapeDtypeStruct((b, rows, d), BF16))
        out_specs.append(row)
    if router is not None:
        w_r, b_r = router
        nr = w_r.shape[1]
        args += [w_r, b_r]
        in_specs += [pl.BlockSpec((d, nr), lambda b_, i: (0, 0)), pl.BlockSpec((1, nr), lambda b_, i: (0, 0))]
        out_shape.append(jax.ShapeDtypeStruct((b, rows, nr), F32))
        out_specs.append(pl.BlockSpec((1, r, nr), lambda b_, i: (b_, i, 0)))
    return pl.pallas_call(
        functools.partial(_postnorm_kernel, alpha=alpha, n_y=len(ys), row_gated=row_gates is not None,
                          with_h=with_h, with_router=router is not None),
        grid=(b, rows // r), in_specs=in_specs, out_specs=out_specs, out_shape=out_shape,
        compiler_params=_params("arbitrary", "arbitrary"), name="postnorm",
    )(*args)


def _mm_kernel(a_ref, b_ref, o_ref):
    o_ref[...] = _dot(a_ref[...].astype(BF16), b_ref[...].astype(BF16)).astype(o_ref.dtype)


def _mm(a, b, out_dtype=F32):
    m, k = a.shape
    n = b.shape[1]
    tm = _pick(m, (512, 256, 128, 64, 8))
    tn = _pick(n, (1024, 896, 768, 640, 512, 384, 256, 128))
    return pl.pallas_call(
        _mm_kernel, grid=(m // tm, n // tn),
        in_specs=[pl.BlockSpec((tm, k), lambda i, j: (i, 0)), pl.BlockSpec((k, tn), lambda i, j: (0, j))],
        out_specs=pl.BlockSpec((tm, tn), lambda i, j: (i, j)),
        out_shape=jax.ShapeDtypeStruct((m, n), out_dtype),
        compiler_params=_params("arbitrary", "arbitrary"), name="mm",
    )(a, b)


def _even_prep_kernel(qkv_ref, qkv_p_ref, qkv_n_ref, xr_ref, xr_p_ref, xr_n_ref, ab_ref,
                      cw_ref, xw_ref, xb_ref, nea_ref, dtb_ref,
                      q_ref, k_ref, v_ref, u_ref, gb_ref, *, n_lat_blocks, n_blocks, heads, r):
    i = pl.program_id(1)
    pv = jnp.where((i != 0) & (i != n_lat_blocks), 1.0, 0.0)
    nv = jnp.where((i != n_lat_blocks - 1) & (i != n_blocks - 1), 1.0, 0.0)
    row = lax.broadcasted_iota(jnp.int32, (r, 1), 0)

    def conv(x, p8, n8, w):
        p8 = p8 * pv
        n8 = n8 * nv
        xm1 = jnp.where(row == 0, p8[7:8], pltpu.roll(x, 1, 0))
        xm2 = jnp.where(row == 0, p8[6:7], jnp.where(row == 1, p8[7:8], pltpu.roll(x, 2, 0)))
        xp1 = jnp.where(row == r - 1, n8[0:1], pltpu.roll(x, r - 1, 0))
        return w[0:1] * xm2 + w[1:2] * xm1 + w[2:3] * x + w[3:4] * xp1

    for j in range(3 * heads):
        sl = slice(j * LANES, (j + 1) * LANES)
        y = _silu(conv(qkv_ref[0, :, sl], qkv_p_ref[0, :, sl], qkv_n_ref[0, :, sl], cw_ref[:, sl]))
        if j < 2 * heads:
            y = y * lax.rsqrt(jnp.sum(y * y, axis=-1, keepdims=True) + NORM_EPS)
        if j < heads:
            q_ref[0, :, sl] = y * (DN_DK ** -0.5)
        elif j < 2 * heads:
            k_ref[0, :, slice((j - heads) * LANES, (j - heads + 1) * LANES)] = y
        else:
            v_ref[0, :, slice((j - 2 * heads) * LANES, (j - 2 * heads + 1) * LANES)] = y
    for j in range(xr_ref.shape[2] // LANES):
        sl = slice(j * LANES, (j + 1) * LANES)
        u_ref[0, :, sl] = conv(xr_ref[0, :, sl], xr_p_ref[0, :, sl], xr_n_ref[0, :, sl], xw_ref[:, sl]) + xb_ref[:, sl]
    ab = ab_ref[0]
    lane = lax.broadcasted_iota(jnp.int32, ab.shape, 1)
    g = nea_ref[...] * _softplus(ab + dtb_ref[...])
    gb_ref[0] = jnp.where(lane < 2 * heads, g, _sigmoid(ab))


def _even_prep(p, conv_qkv, conv_x_w, conv_x_b, neg_exp_a, dt_bias, r, n_lat, heads, width):
    b, s, _ = p.shape
    vw = heads * DN_DV
    qkvw = 3 * vw
    assert width == vw and qkvw % width == 0
    nb, nlb, r8 = s // r, n_lat // r, r // 8
    n8 = s // 8

    def cur(wd, cb):
        return pl.BlockSpec((1, r, wd), lambda b_, i: (b_, i, cb))

    def prev(wd, cb):
        return pl.BlockSpec((1, 8, wd), lambda b_, i: (b_, jnp.maximum(i * r8 - 1, 0), cb))

    def nxt(wd, cb):
        return pl.BlockSpec((1, 8, wd), lambda b_, i: (b_, jnp.minimum((i + 1) * r8, n8 - 1), cb))

    def par(shape):
        return pl.BlockSpec(shape, lambda b_, i: (0, 0))

    xcb = (qkvw + vw) // width
    abcb = (qkvw + vw + 2 * width) // LANES
    pad = LANES - 2 * heads
    nea = jnp.pad(neg_exp_a.reshape(1, 2 * heads), ((0, 0), (0, pad)))
    dtb = jnp.pad(dt_bias.reshape(1, 2 * heads), ((0, 0), (0, pad)))
    outs = [jax.ShapeDtypeStruct((b, s, vw), F32)] * 3 + [jax.ShapeDtypeStruct((b, s, width), F32),
                                                            jax.ShapeDtypeStruct((b, s, LANES), F32)]
    return pl.pallas_call(
        functools.partial(_even_prep_kernel, n_lat_blocks=nlb, n_blocks=nb, heads=heads, r=r),
        grid=(b, nb),
        in_specs=[cur(qkvw, 0), prev(qkvw, 0), nxt(qkvw, 0), cur(width, xcb), prev(width, xcb), nxt(width, xcb),
                  cur(LANES, abcb), par((4, qkvw)), par((4, width)), par((1, width)), par((1, LANES)), par((1, LANES))],
        out_specs=[cur(vw, 0), cur(vw, 0), cur(vw, 0), cur(width, 0), cur(LANES, 0)],
        out_shape=outs, compiler_params=_params("arbitrary", "arbitrary"), name="even_prep",
    )(p, p, p, p, p, p, p, conv_qkv, conv_x_w, conv_x_b.reshape(1, width), nea, dtb)


def _split2(x):
    hi = x.astype(BF16)
    return hi, (x - hi.astype(F32)).astype(BF16)


DN_CHUNKS_PER_STEP = 2
DN_PACK = 4


def _block_diag(x, row_blk, lane_blk, pack):
    tiled = jnp.concatenate([x] * pack, axis=0)
    return jnp.where(row_blk == lane_blk, tiled, 0.0).astype(BF16)


def _dot_hl(a, b_bd):
    n = a[0].shape[0]
    r = _dot(jnp.concatenate([a[0], a[1]], axis=0), b_bd)
    return r[:n] + r[n:]


def _delta_local_kernel(q_ref, k_ref, v_ref, g_ref, u_ref, wq_ref, ak_ref, gt_ref, *, heads, pack, cpb):
    c = DN_CHUNK
    wa = pack * c
    wk = pack * LANES
    ri = lax.broadcasted_iota(jnp.int32, (c, wa), 0)
    la = lax.broadcasted_iota(jnp.int32, (c, wa), 1)
    ci = la & (c - 1)
    blk_a = la >> 6
    blk_k = lax.broadcasted_iota(jnp.int32, (c, wk), 1) >> 7
    rb_a = lax.broadcasted_iota(jnp.int32, (wa, wa), 0) >> 6
    lb_a = lax.broadcasted_iota(jnp.int32, (wa, wa), 1) >> 6
    rb_k = lax.broadcasted_iota(jnp.int32, (wa, wk), 0) >> 6
    lb_k = lax.broadcasted_iota(jnp.int32, (wa, wk), 1) >> 7
    eye = jnp.where(ri == ci, 1.0, 0.0)
    ri1 = lax.broadcasted_iota(jnp.int32, (c, c), 0)
    ci1 = lax.broadcasted_iota(jnp.int32, (c, c), 1)

    def bd_a(x):
        return _block_diag(x, rb_a, lb_a, pack)

    def bd_k(x):
        return _block_diag(x, rb_k, lb_k, pack)

    def per_head(cols, blk):
        out = cols[0]
        for t in range(1, pack):
            out = jnp.where(blk == t, cols[t], out)
        return out

    incl = [(ri >= ci), (ri <= ci)]
    strict = [(ri > ci), (ri < ci)]
    last = [c - 1, 0]
    levels = [[strict[d] & ((ri >> (k + 1)) == (ci >> (k + 1))) & ((ri >> k) != (ci >> k)) for k in range(6)]
              for d in range(2)]
    ones = [jnp.where(ri1 >= ci1, 1.0, 0.0).astype(BF16), jnp.where(ri1 <= ci1, 1.0, 0.0).astype(BF16)]

    packs = []
    for j in range(cpb):
        rows = slice(j * c, (j + 1) * c)
        gall = g_ref[0, rows]
        g1 = gall.astype(BF16)
        rem = gall - g1.astype(F32)
        g2 = rem.astype(BF16)
        g3 = (rem - g2.astype(F32)).astype(BF16)
        for d in range(2):
            gcum = _dot(ones[d], g1) + (_dot(ones[d], g2) + _dot(ones[d], g3))
            gt_ref[d, j] = jnp.exp(gcum[last[d]:last[d] + 1, :])
            for h0 in range(0, heads, pack):
                cols = [d * heads + h0 + t for t in range(pack)]
                gcs = [gcum[:, cc:cc + 1] for cc in cols]
                gc_a = per_head([jnp.broadcast_to(x, (c, wa)) for x in gcs], blk_a)
                gr_a = jnp.sum(jnp.where(ri == ci, gc_a, 0.0), axis=0, keepdims=True)
                decay = jnp.where(incl[d], jnp.exp(jnp.where(incl[d], gc_a - gr_a, 0.0)), 0.0)
                beta_k = per_head([jnp.broadcast_to(gall[:, 2 * heads + cc:2 * heads + cc + 1], (c, wk))
                                   for cc in cols], blk_k)
                gc_k = per_head([jnp.broadcast_to(x, (c, wk)) for x in gcs], blk_k)
                gl_k = per_head([jnp.broadcast_to(x[last[d]:last[d] + 1], (c, wk)) for x in gcs], blk_k)
                sl = slice(h0 * LANES, (h0 + pack) * LANES)
                q_k, k_k, v_k = q_ref[0, rows, sl], k_ref[0, rows, sl], v_ref[0, rows, sl]
                kb_k = k_k * beta_k
                eg_k = jnp.exp(gc_k)
                packs.append(dict(j=j, d=d, h0=h0, decay=decay, kb=kb_k, q=q_k, k=k_k, eg=eg_k,
                                  rhs_u=v_k * beta_k, rhs_w=kb_k * eg_k, kd=k_k * jnp.exp(gl_k - gc_k)))

    for p in packs:
        lhs = jnp.concatenate([p["kb"], p["q"]], axis=0).astype(BF16)
        p["aa"] = _dot_nt(lhs, bd_k(p["k"]))
    for p in packs:
        d = p["d"]
        p["a"] = jnp.where(strict[d], p["aa"][:c] * p["decay"], 0.0).astype(BF16).astype(F32)
        p["a_qk"] = jnp.where(incl[d], p["aa"][c:] * p["decay"], 0.0)
        p["t"] = eye - jnp.where(levels[d][0], p["a"], 0.0)
    for k in range(1, 6):
        for p in packs:
            p["ts"] = _split2(p["t"])
            p["x"] = _dot_hl(p["ts"], bd_a(jnp.where(levels[p["d"]][k], p["a"], 0.0)))
        for p in packs:
            xs = _split2(p["x"])
            y = _dot_hl(xs, bd_a(p["ts"][0].astype(F32))) + _dot(xs[0], bd_a(p["ts"][1].astype(F32)))
            p["t"] = p["t"] - y
    for p in packs:
        p["ts"] = _split2(p["t"])
    for name in ("u", "w"):
        for p in packs:
            rh, rl = _split2(p["rhs_" + name])
            p[name] = _dot_hl(p["ts"], bd_k(rh.astype(F32))) + _dot(p["ts"][0], bd_k(rl.astype(F32)))
    for p in packs:
        j, d = p["j"], p["d"]
        qd_k = p["q"] * p["eg"]
        for t in range(pack):
            h = p["h0"] + t
            ks = slice(t * LANES, (t + 1) * LANES)
            u_ref[d, j, h] = p["u"][:, ks]
            wq_ref[d, j, h, 0:c] = p["w"][:, ks].astype(BF16)
            wq_ref[d, j, h, c:2 * c] = qd_k[:, ks].astype(BF16)
            ak_ref[d, j, h, 0:c] = p["a_qk"][:, t * c:(t + 1) * c].astype(BF16)
            ak_ref[d, j, h, c:c + DN_DK] = p["kd"][:, ks].T.astype(BF16)


def _delta_scan_kernel(uf, wqf, akf, gtf, ub, wqb, akb, gtb, of, ob, state, *, heads, cpb):
    c = DN_CHUNK

    @pl.when(pl.program_id(1) == 0)
    def _():
        state[...] = jnp.zeros_like(state)

    dirs = ((uf, wqf, akf, gtf, of), (ub, wqb, akb, gtb, ob))
    chains = [(d, h) for d in range(2) for h in range(heads)]
    s_cur = {ch: state[ch[0], ch[1]] for ch in chains}
    for step in range(cpb):
        at = (step, cpb - 1 - step)
        r1 = {(d, h): _dot(dirs[d][1][at[d], h], s_cur[(d, h)].astype(BF16)) for d, h in chains}
        r2 = {}
        for d, h in chains:
            v_new = dirs[d][0][at[d], h] - r1[(d, h)][:c]
            r2[(d, h)] = _dot(dirs[d][2][at[d], h], v_new.astype(BF16))
        for d, h in chains:
            gt = dirs[d][3][at[d]]
            dirs[d][4][0, at[d] * c:(at[d] + 1) * c, h * LANES:(h + 1) * LANES] = r1[(d, h)][c:] + r2[(d, h)][:c]
            s_cur[(d, h)] = s_cur[(d, h)] * gt[:, d * heads + h:d * heads + h + 1] + r2[(d, h)][c:]
    for d, h in chains:
        state[d, h] = s_cur[(d, h)]


def _delta(q, k, v, gates, n_lat, heads):
    b, s, vw = q.shape
    c = DN_CHUNK
    n_chunks, n_lat_c = s // c, n_lat // c
    n_ctx_c = n_chunks - n_lat_c
    cpb = _pick(math.gcd(n_lat_c, n_ctx_c), (DN_CHUNKS_PER_STEP, 1))
    seq = pl.BlockSpec((1, cpb * c, vw), lambda b_, i: (b_, i, 0))

    def loc(rows, cols):
        return pl.BlockSpec((2, None, cpb, heads, rows, cols), lambda b_, i: (0, b_, i, 0, 0, 0))

    u, wq, ak, gt = pl.pallas_call(
        functools.partial(_delta_local_kernel, heads=heads, pack=math.gcd(heads, DN_PACK), cpb=cpb),
        grid=(b, n_chunks // cpb),
        in_specs=[seq, seq, seq, pl.BlockSpec((1, cpb * c, LANES), lambda b_, i: (b_, i, 0))],
        out_specs=[loc(c, DN_DV), loc(2 * c, DN_DV), loc(c + DN_DK, c),
                   pl.BlockSpec((2, None, cpb, 1, LANES), lambda b_, i: (0, b_, i, 0, 0))],
        out_shape=[jax.ShapeDtypeStruct((2, b, n_chunks, heads, c, DN_DV), F32),
                   jax.ShapeDtypeStruct((2, b, n_chunks, heads, 2 * c, DN_DV), BF16),
                   jax.ShapeDtypeStruct((2, b, n_chunks, heads, c + DN_DK, c), BF16),
                   jax.ShapeDtypeStruct((2, b, n_chunks, 1, LANES), F32)],
        compiler_params=_params("arbitrary", "arbitrary"), name="delta_local",
    )(q, k, v, gates)

    n_blk, n_lat_b = n_chunks // cpb, n_lat_c // cpb
    n_ctx_b = n_blk - n_lat_b

    def fwd(st):
        return jnp.where(st < n_ctx_b, n_lat_b + st, st - n_ctx_b)

    def bwd(st):
        return jnp.where(st < n_ctx_b, n_blk - 1 - st, n_lat_b - 1 - (st - n_ctx_b))

    def chunk(d, order, rows, cols):
        return pl.BlockSpec((None, None, cpb, heads, rows, cols), lambda b_, st: (d, b_, order(st), 0, 0, 0))

    def gspec(d, order):
        return pl.BlockSpec((None, None, cpb, 1, LANES), lambda b_, st: (d, b_, order(st), 0, 0))

    def ospec(order):
        return pl.BlockSpec((1, cpb * c, vw), lambda b_, st: (b_, order(st), 0))

    ins, args = [], []
    for d, order in ((0, fwd), (1, bwd)):
        ins += [chunk(d, order, c, DN_DV), chunk(d, order, 2 * c, DN_DV), chunk(d, order, c + DN_DK, c),
                gspec(d, order)]
        args += [u, wq, ak, gt]
    out = jax.ShapeDtypeStruct((b, s, vw), F32)
    return pl.pallas_call(
        functools.partial(_delta_scan_kernel, heads=heads, cpb=cpb), grid=(b, n_blk),
        in_specs=ins, out_specs=[ospec(fwd), ospec(bwd)], out_shape=[out, out],
        scratch_shapes=[pltpu.VMEM((2, heads, DN_DK, DN_DV), F32)],
        compiler_params=_params("arbitrary", "arbitrary"), name="delta_scan",
    )(*args)


def _lru_kernel(uf_ref, ub_ref, wri_ref, bri_ref, spl_ref, hf_ref, hb_ref, carry, *, r, n_blk):
    @pl.when(pl.program_id(1) == 0)
    def _():
        carry[...] = jnp.zeros_like(carry)

    row = lax.broadcasted_iota(jnp.int32, (r, 1), 0)
    for d, (u_ref, h_ref) in enumerate(((uf_ref, hf_ref), (ub_ref, hb_ref))):
        for n in range(n_blk):
            sl = slice(n * LANES, (n + 1) * LANES)
            u = u_ref[0, :, sl]
            ri = _dot(u.astype(BF16), wri_ref[d, n]) + bri_ref[d, n]
            rg = _sigmoid(ri[:, :LANES])
            ig = _sigmoid(ri[:, LANES:])
            log_a = -spl_ref[d, n] * rg
            a = jnp.exp(log_a)
            bt = jnp.sqrt(-_expm1_nonpos(2.0 * log_a)) * (ig * u)
            sh = 1
            while sh < r:
                if d == 0:
                    keep = row >= sh
                    a_s, b_s = pltpu.roll(a, sh, 0), pltpu.roll(bt, sh, 0)
                else:
                    keep = row < r - sh
                    a_s, b_s = pltpu.roll(a, r - sh, 0), pltpu.roll(bt, r - sh, 0)
                bt = a * jnp.where(keep, b_s, 0.0) + bt
                a = a * jnp.where(keep, a_s, 1.0)
                sh *= 2
            h = bt + a * carry[d, :, sl]
            h_ref[0, :, sl] = h
            carry[d, :, sl] = h[r - 1:r] if d == 0 else h[0:1]


def _lru(u, w_ri, b_ri, spl, r, n_lat):
    b, s, width = u.shape
    n_blk = width // LANES
    nb, nlb = s // r, n_lat // r
    ncb = nb - nlb

    def fwd(st):
        return jnp.where(st < ncb, nlb + st, st - ncb)

    def bwd(st):
        return jnp.where(st < ncb, nb - 1 - st, nlb - 1 - (st - ncb))

    def seq(order):
        return pl.BlockSpec((1, r, width), lambda b_, st: (b_, order(st), 0))

    def par(shape):
        return pl.BlockSpec(shape, lambda b_, st: (0,) * len(shape))

    out = jax.ShapeDtypeStruct((b, s, width), F32)
    return pl.pallas_call(
        functools.partial(_lru_kernel, r=r, n_blk=n_blk), grid=(b, nb),
        in_specs=[seq(fwd), seq(bwd), par(w_ri.shape), par(b_ri.shape), par(spl.shape)],
        out_specs=[seq(fwd), seq(bwd)], out_shape=[out, out],
        scratch_shapes=[pltpu.VMEM((2, 1, width), F32)],
        compiler_params=_params("arbitrary", "arbitrary"), name="lru",
    )(u, u, w_ri, b_ri, spl)


def _gelu_tanh(x):
    return 0.5 * x * (1.0 + jnp.tanh(math.sqrt(2.0 / math.pi) * (x + 0.044715 * (x * x * x))))


def _even_out_kernel(of_ref, ob_ref, z_ref, hf_ref, hb_ref, y_ref, on_ref, mix_ref, *, heads, n_blk):
    for h in range(heads):
        sl = slice(h * LANES, (h + 1) * LANES)
        o = of_ref[0, :, sl] + ob_ref[0, :, sl]
        o = o * lax.rsqrt(jnp.mean(o * o, axis=-1, keepdims=True) + NORM_EPS) * on_ref[...]
        mix_ref[0, :, sl] = (o * _silu(z_ref[0, :, sl])).astype(BF16)
    for n in range(n_blk):
        sl = slice(n * LANES, (n + 1) * LANES)
        osl = slice((heads + n) * LANES, (heads + n + 1) * LANES)
        mix_ref[0, :, osl] = ((hf_ref[0, :, sl] + hb_ref[0, :, sl]) * _gelu_tanh(y_ref[0, :, sl])).astype(BF16)


def _even_out(o_f, o_b, p, h_f, h_b, o_norm, r, heads):
    b, s, vw = o_f.shape
    width = h_f.shape[2]

    def cur(wd, cb):
        return pl.BlockSpec((1, r, wd), lambda b_, i: (b_, i, cb))

    zcb = (3 * vw) // vw
    ycb = (4 * vw + width) // width
    return pl.pallas_call(
        functools.partial(_even_out_kernel, heads=heads, n_blk=width // LANES), grid=(b, s // r),
        in_specs=[cur(vw, 0), cur(vw, 0), cur(vw, zcb), cur(width, 0), cur(width, 0), cur(width, ycb),
                  pl.BlockSpec((1, DN_DV), lambda b_, i: (0, 0))],
        out_specs=cur(vw + width, 0), out_shape=jax.ShapeDtypeStruct((b, s, vw + width), BF16),
        compiler_params=_params("arbitrary", "arbitrary"), name="even_out",
    )(o_f, o_b, p, h_f, h_b, p, o_norm.reshape(1, DN_DV))


def _rope_pair(y, tab):
    y = y * tab
    return y + pltpu.roll(y, MLA_ROPE, 1)


def _q_proj_kernel(a_ref, w_ref, tab_ref, q_ref, *, scale):
    acc = _dot(a_ref[0], w_ref[0])
    q_ref[0, 0, :, 0:MLA_NOPE] = (acc[:, :MLA_NOPE] * scale).astype(BF16)
    qr = _rope_pair(acc[:, MLA_NOPE:], tab_ref[...])
    q_ref[0, 0, :, MLA_NOPE:MLA_QD] = (qr[:, :MLA_ROPE] * scale).astype(BF16)


def _q_proj(h, w_q, tab, heads):
    b, s, d = h.shape
    tm = _pick(s, (768, 512, 640, 256, 128))
    return pl.pallas_call(
        functools.partial(_q_proj_kernel, scale=math.log2(math.e) * MLA_QD ** -0.5), grid=(b, s // tm, heads),
        in_specs=[pl.BlockSpec((1, tm, d), lambda b_, i, hd: (b_, i, 0)),
                  pl.BlockSpec((1, d, 2 * LANES), lambda b_, i, hd: (hd, 0, 0)),
                  pl.BlockSpec((tm, LANES), lambda b_, i, hd: (i, 0))],
        out_specs=pl.BlockSpec((1, 1, tm, MLA_QD), lambda b_, i, hd: (b_, hd, i, 0)),
        out_shape=jax.ShapeDtypeStruct((b, heads, s, MLA_QD), BF16),
        compiler_params=_params("arbitrary", "arbitrary", "arbitrary"), name="q_proj",
    )(h, w_q, tab)


def _ckv_proj_kernel(a_ref, w_ref, tab_ref, g_ref, ckv_ref, kr_ref, *, rank):
    acc = _dot(a_ref[0], w_ref[...])
    ckv = acc[:, :rank]
    ckv = ckv * lax.rsqrt(jnp.mean(ckv * ckv, axis=-1, keepdims=True) + NORM_EPS) * g_ref[...]
    ckv_ref[0] = ckv.astype(BF16)
    kr_ref[0] = _rope_pair(acc[:, rank:], tab_ref[...]).astype(BF16)


def _ckv_proj(h, w_c, tab, kv_norm):
    b, s, d = h.shape
    rank = kv_norm.shape[0]
    tm = _pick(s, (768, 512, 640, 256, 128))
    return pl.pallas_call(
        functools.partial(_ckv_proj_kernel, rank=rank), grid=(b, s // tm),
        in_specs=[pl.BlockSpec((1, tm, d), lambda b_, i: (b_, i, 0)),
                  pl.BlockSpec((d, rank + LANES), lambda b_, i: (0, 0)),
                  pl.BlockSpec((tm, LANES), lambda b_, i: (i, 0)),
                  pl.BlockSpec((1, rank), lambda b_, i: (0, 0))],
        out_specs=[pl.BlockSpec((1, tm, rank), lambda b_, i: (b_, i, 0)),
                   pl.BlockSpec((1, tm, LANES), lambda b_, i: (b_, i, 0))],
        out_shape=[jax.ShapeDtypeStruct((b, s, rank), BF16), jax.ShapeDtypeStruct((b, s, LANES), BF16)],
        compiler_params=_params("arbitrary", "arbitrary"), name="ckv_proj",
    )(h, w_c, tab, kv_norm.reshape(1, rank))


def _kv_up_kernel(a_ref, w_ref, kr_ref, k_ref, v_ref):
    acc = _dot(a_ref[0], w_ref[0])
    k_ref[0, 0, :, 0:MLA_NOPE] = acc[:, :MLA_NOPE].astype(BF16)
    k_ref[0, 0, :, MLA_NOPE:MLA_QD] = kr_ref[0, :, 0:MLA_ROPE]
    v_ref[0, 0] = acc[:, MLA_NOPE:].astype(BF16)


def _kv_up(ckv, w_ukv, kr, heads):
    b, s, rank = ckv.shape
    tm = _pick(s, (768, 512, 640, 256, 128))
    return pl.pallas_call(
        _kv_up_kernel, grid=(b, s // tm, heads),
        in_specs=[pl.BlockSpec((1, tm, rank), lambda b_, i, hd: (b_, i, 0)),
                  pl.BlockSpec((1, rank, MLA_NOPE + MLA_V), lambda b_, i, hd: (hd, 0, 0)),
                  pl.BlockSpec((1, tm, LANES), lambda b_, i, hd: (b_, i, 0))],
        out_specs=[pl.BlockSpec((1, 1, tm, MLA_QD), lambda b_, i, hd: (b_, hd, i, 0)),
                   pl.BlockSpec((1, 1, tm, MLA_V), lambda b_, i, hd: (b_, hd, i, 0))],
        out_shape=[jax.ShapeDtypeStruct((b, heads, s, MLA_QD), BF16),
                   jax.ShapeDtypeStruct((b, heads, s, MLA_V), BF16)],
        compiler_params=_params("arbitrary", "arbitrary", "arbitrary"), name="kv_up",
    )(ckv, w_ukv, kr)


def _flash_kernel(q_ref, k_ref, v_ref, o_ref, m_ref, l_ref, acc_ref, *, sub):
    j = pl.program_id(3)

    @pl.when(j == 0)
    def _():
        m_ref[...] = jnp.full_like(m_ref, -jnp.inf)
        l_ref[...] = jnp.zeros_like(l_ref)
        acc_ref[...] = jnp.zeros_like(acc_ref)

    k = k_ref[0, 0]
    v = v_ref[0, 0]
    for c in range(q_ref.shape[2] // sub):
        rows = slice(c * sub, (c + 1) * sub)
        s = _dot_nt(q_ref[0, 0, rows, :], k)
        m_prev = m_ref[rows]
        m_new = jnp.maximum(m_prev, jnp.max(s, axis=-1, keepdims=True))
        p = jnp.exp2(s - m_new)
        alpha = jnp.exp2(m_prev - m_new)
        l_ref[rows] = alpha * l_ref[rows] + jnp.sum(p, axis=-1, keepdims=True)
        acc_ref[rows] = alpha * acc_ref[rows] + _dot(p.astype(BF16), v)
        m_ref[rows] = m_new

    @pl.when(j == pl.num_programs(3) - 1)
    def _():
        o_ref[0] = (acc_ref[...] / l_ref[...]).astype(o_ref.dtype)


def _attn_full_kernel(q_ref, k_ref, v_ref, o_ref, *, sub):
    k = k_ref[0, 0]
    v = v_ref[0, 0]
    for c in range(q_ref.shape[2] // sub):
        rows = slice(c * sub, (c + 1) * sub)
        s = _dot_nt(q_ref[0, 0, rows, :], k)
        p = jnp.exp2(s - jnp.max(s, axis=-1, keepdims=True))
        l = jnp.sum(p, axis=-1, keepdims=True)
        o_ref[0, rows, :] = (_dot(p.astype(BF16), v) / l).astype(o_ref.dtype)


def _flash(q, k, v, q_rows, kv_rows):
    b, heads, _, dq = q.shape
    dv = v.shape[3]
    q0, nq = q_rows
    k0, nk = kv_rows
    tq = _pick(math.gcd(nq, q0) if q0 else nq, (2048, 1024, 512, 256, 128))
    tk = _pick(math.gcd(nk, k0) if k0 else nk, (nk, 768, 512, 256, 128))
    qo, ko = q0 // tq, k0 // tk
    if nk == tk:
        return pl.pallas_call(
            functools.partial(_attn_full_kernel, sub=min(tq, FLASH_SUB)), grid=(b, heads, nq // tq),
            in_specs=[pl.BlockSpec((1, 1, tq, dq), lambda b_, h, i: (b_, h, qo + i, 0)),
                      pl.BlockSpec((1, 1, tk, dq), lambda b_, h, i: (b_, h, ko, 0)),
                      pl.BlockSpec((1, 1, tk, dv), lambda b_, h, i: (b_, h, ko, 0))],
            out_specs=pl.BlockSpec((1, tq, dv), lambda b_, h, i: (b_, i, h)),
            out_shape=jax.ShapeDtypeStruct((b, nq, heads * dv), BF16),
            compiler_params=_params("arbitrary", "arbitrary", "arbitrary"), name="attn_full",
        )(q, k, v)
    return pl.pallas_call(
        functools.partial(_flash_kernel, sub=min(tq, FLASH_SUB)), grid=(b, heads, nq // tq, nk // tk),
        in_specs=[pl.BlockSpec((1, 1, tq, dq), lambda b_, h, i, j: (b_, h, qo + i, 0)),
                  pl.BlockSpec((1, 1, tk, dq), lambda b_, h, i, j: (b_, h, ko + j, 0)),
                  pl.BlockSpec((1, 1, tk, dv), lambda b_, h, i, j: (b_, h, ko + j, 0))],
        out_specs=pl.BlockSpec((1, tq, dv), lambda b_, h, i, j: (b_, i, h)),
        out_shape=jax.ShapeDtypeStruct((b, nq, heads * dv), BF16),
        scratch_shapes=[pltpu.VMEM((tq, 1), F32), pltpu.VMEM((tq, 1), F32), pltpu.VMEM((tq, dv), F32)],
        compiler_params=_params("arbitrary", "arbitrary", "arbitrary", "arbitrary"), name="flash",
    )(q, k, v)


def _experts_kernel(be_ref, nu_ref, x_ref, w1_ref, w3_ref, w2_ref, y_ref):
    @pl.when(pl.program_id(0) < nu_ref[0])
    def _():
        x = x_ref[...].astype(BF16)
        h1 = _dot(x, w1_ref[0].astype(BF16))
        h3 = _dot(x, w3_ref[0].astype(BF16))
        hh = (_silu(h1) * h3).astype(BF16)
        y_ref[...] = _dot(hh, w2_ref[0].astype(BF16)).astype(y_ref.dtype)


def _experts(xg, blk_e, n_used, w1, w3, w2, layer, tm):
    rows, d = xg.shape
    de = w1.shape[3]
    def used(i, nu):
        return jnp.minimum(i, nu[0] - 1)

    grid_spec = pltpu.PrefetchScalarGridSpec(
        num_scalar_prefetch=2, grid=(rows // tm,),
        in_specs=[pl.BlockSpec((tm, d), lambda i, be, nu: (used(i, nu), 0)),
                  pl.BlockSpec((None, 1, d, de), lambda i, be, nu: (layer, be[used(i, nu)], 0, 0)),
                  pl.BlockSpec((None, 1, d, de), lambda i, be, nu: (layer, be[used(i, nu)], 0, 0)),
                  pl.BlockSpec((None, 1, de, d), lambda i, be, nu: (layer, be[used(i, nu)], 0, 0))],
        out_specs=pl.BlockSpec((tm, d), lambda i, be, nu: (used(i, nu), 0)))
    return pl.pallas_call(
        _experts_kernel, grid_spec=grid_spec, out_shape=jax.ShapeDtypeStruct((rows, d), BF16),
        compiler_params=_params("arbitrary"), name="experts",
    )(blk_e, n_used, xg, w1, w3, w2)


def _moe(h, logits, n_groups, n_experts, w1, w3, w2, layer, tm):
    n_tok, d = h.shape
    epg = n_experts // n_groups
    lg = logits[:, :n_groups]
    grp = jnp.argmax(lg, axis=-1)
    p_grp = jnp.take_along_axis(jax.nn.softmax(lg, -1), grp[:, None], -1)
    le = logits[:, n_groups:n_groups + n_experts].reshape(n_tok, n_groups, epg)
    le = jnp.take_along_axis(le, grp[:, None, None], axis=1)[:, 0]
    top_v, top_i = lax.top_k(le, TOP_K)
    gate = p_grp * jax.nn.softmax(top_v, -1)
    expert = (grp[:, None] * epg + top_i).reshape(-1).astype(jnp.int32)
    onehot = (expert[:, None] == jnp.arange(n_experts, dtype=jnp.int32)[None, :]).astype(jnp.int32)
    rank = jnp.sum((jnp.cumsum(onehot, axis=0) - onehot) * onehot, axis=1)
    counts = jnp.sum(onehot, axis=0)
    padded = (counts + tm - 1) // tm * tm
    pend = jnp.cumsum(padded)
    dest = (pend - padded)[expert] + rank
    n_blk = (n_tok * TOP_K + tm - 1) // tm + n_experts
    blk_e = jnp.minimum(jnp.sum(pend[None, :] <= (jnp.arange(n_blk) * tm)[:, None], axis=1), n_experts - 1)
    n_used = (pend[-1] // tm).reshape(1)
    src = (jnp.arange(n_blk * tm, dtype=jnp.int32) % n_tok).at[dest].set(
        jnp.arange(n_tok * TOP_K, dtype=jnp.int32) // TOP_K)
    y = _experts(h[src], blk_e.astype(jnp.int32), n_used.astype(jnp.int32), w1, w3, w2, layer, tm)
    dest = dest.reshape(n_tok, TOP_K)
    return (y[dest[:, 0]], y[dest[:, 1]]), (gate[:, 0:1], gate[:, 1:2])


def _rope_table(n_lat, n_ctx):
    rows = n_lat // GRID_W
    row = jnp.broadcast_to(jnp.arange(rows)[:, None], (rows, GRID_W)).reshape(-1)
    col = jnp.broadcast_to(jnp.arange(GRID_W)[None, :], (rows, GRID_W)).reshape(-1)
    pos = jnp.stack([row, col], -1).astype(F32)
    inv = ROPE_BASE ** (-jnp.arange(ROPE_FREQS, dtype=F32) / ROPE_FREQS)
    ang = pos[:, :, None] * inv
    ang = jnp.broadcast_to(ang[:, :, None, :], (n_lat, 2, 2, ROPE_FREQS)).reshape(n_lat, MLA_ROPE)
    lat = jnp.concatenate([jnp.cos(ang), jnp.sin(ang)], axis=-1)
    ctx = jnp.concatenate([jnp.ones((n_ctx, MLA_ROPE), F32), jnp.zeros((n_ctx, MLA_ROPE), F32)], axis=-1)
    return jnp.concatenate([lat, ctx], axis=0)


def _rot_cols(w):
    wr = w.reshape(w.shape[:-1] + (2, 2, ROPE_FREQS))
    return jnp.stack([-wr[..., 1, :], wr[..., 0, :]], axis=-2).reshape(w.shape)


def kernel(x, c, ctx, c_ctx, ada_w, ada_b, ln_mix_g, ln_mix_b, ln_ffn_g, ln_ffn_b, ev_w_in, ev_conv_qkv, ev_a_log,
           ev_dt_bias, ev_o_norm, ev_conv_x_w, ev_conv_x_b, ev_w_r, ev_b_r, ev_w_i, ev_b_i, ev_lam, ev_w_out,
           od_w_in, od_kv_norm, od_w_ukv, od_w_out, moe_w_grp, moe_b_grp, moe_w_exp, moe_b_exp, moe_w1, moe_w3,
           moe_w2):
    bsz, n_lat, d = x.shape
    n_ctx = ctx.shape[1]
    s = n_lat + n_ctx
    depth = ada_w.shape[0]
    alpha = (2.0 * depth) ** 0.25
    r = min(256, n_ctx)
    assert n_lat % r == 0 and n_ctx % r == 0 and n_lat % GRID_W == 0 and bsz + 1 <= 8
    dn_heads = ev_a_log.shape[-1]
    vw = dn_heads * DN_DV
    width = ev_lam.shape[-1]
    lru_blocks = ev_w_r.shape[2]
    assert width // lru_blocks == LANES
    rank = od_kv_norm.shape[-1]
    mla_heads = od_w_ukv.shape[-1] // (MLA_NOPE + MLA_V)
    n_groups, n_experts = moe_w_grp.shape[-1], moe_w_exp.shape[-1]
    moe_tm = 256 if (bsz * s * TOP_K) // n_experts >= 512 else 128

    xs = jnp.concatenate([x, ctx], axis=1)
    c_all = jnp.zeros((8, d), F32).at[:bsz].set(c).at[bsz].set(c_ctx)
    mods = _adaln(c_all, ada_w, ada_b).reshape(depth, 8, 6, d)

    def seg_tab(layer, k):
        lat = mods[layer, :bsz, k]
        ctx_v = jnp.broadcast_to(mods[layer, bsz, k][None], (bsz, d))
        return jnp.stack([lat, ctx_v], axis=1)[:, :, None, :]

    rope_tab = _rope_table(n_lat, n_ctx)
    hmod = _modulate(xs, seg_tab(0, 0), seg_tab(0, 1), r, n_lat)
    for layer in range(depth):
        i = layer // 2
        last = layer == depth - 1
        if layer % 2 == 0:
            qkvw = 3 * vw
            w = ev_w_in[i]
            o_z, o_a, o_b, o_x, o_y = qkvw, qkvw + vw, qkvw + vw + 2 * dn_heads, qkvw + vw + 4 * dn_heads, \
                qkvw + vw + 4 * dn_heads + width
            w_perm = jnp.concatenate([w[:, :o_a], w[:, o_x:], w[:, o_a:o_x],
                                      jnp.zeros((d, LANES - 4 * dn_heads), F32)], axis=1).astype(BF16)
            p = _mm(hmod.reshape(bsz * s, d), w_perm).reshape(bsz, s, -1)
            q, k, v, u, gb = _even_prep(p, ev_conv_qkv[i], ev_conv_x_w[i], ev_conv_x_b[i], -jnp.exp(ev_a_log[i]),
                                        ev_dt_bias[i], r, n_lat, dn_heads, width)
            o_f, o_b = _delta(q, k, v, gb, n_lat, dn_heads)
            w_ri = jnp.concatenate([ev_w_r[i], ev_w_i[i]], axis=-1).astype(BF16)
            b_ri = jnp.concatenate([ev_b_r[i].reshape(2, lru_blocks, 1, LANES),
                                    ev_b_i[i].reshape(2, lru_blocks, 1, LANES)], axis=-1)
            spl = (LRU_C * jax.nn.softplus(-ev_lam[i])).reshape(2, lru_blocks, 1, LANES)
            h_f, h_b = _lru(u, w_ri, b_ri, spl, r, n_lat)
            mix = _even_out(o_f, o_b, p, h_f, h_b, ev_o_norm[i], r, dn_heads)
            w_out = ev_w_out[i]
        else:
            w = od_w_in[i]
            nq = mla_heads * MLA_QD
            wq = w[:, :nq].reshape(d, mla_heads, MLA_QD)
            wq = jnp.concatenate([wq, _rot_cols(wq[..., MLA_NOPE:])], axis=-1).transpose(1, 0, 2).astype(BF16)
            w_kr = w[:, nq + rank:]
            w_c = jnp.concatenate([w[:, nq:nq + rank], w_kr, _rot_cols(w_kr)], axis=-1).astype(BF16)
            w_u = od_w_ukv[i].reshape(rank, mla_heads, MLA_NOPE + MLA_V).transpose(1, 0, 2).astype(BF16)
            qh = _q_proj(hmod, wq, rope_tab, mla_heads)
            ckv, kr = _ckv_proj(hmod, w_c, rope_tab, od_kv_norm[i])
            kh, vh = _kv_up(ckv, w_u, kr, mla_heads)
            mix = jnp.concatenate([_flash(qh, kh, vh, (0, n_lat), (0, s)),
                                   _flash(qh, kh, vh, (n_lat, n_ctx), (n_lat, n_ctx))], axis=1)
            w_out = od_w_out[i]
        y = _mm(mix.reshape(bsz * s, -1), w_out.astype(BF16), BF16).reshape(bsz, s, d)
        nr = LANES * ((n_groups + n_experts + LANES - 1) // LANES)
        w_rt = jnp.pad(jnp.concatenate([moe_w_grp[layer], moe_w_exp[layer]], axis=1),
                       ((0, 0), (0, nr - n_groups - n_experts)))
        b_rt = jnp.pad(jnp.concatenate([moe_b_grp[layer], moe_b_exp[layer]]), (0, nr - n_groups - n_experts))
        xs, hf, logits = _postnorm(xs, [y], seg_tab(layer, 2), ln_mix_g[layer], ln_mix_b[layer], r, n_lat, alpha,
                                   shift=seg_tab(layer, 3), scale=seg_tab(layer, 4), router=(w_rt, b_rt[None]))
        ys, gts = _moe(hf.reshape(bsz * s, d), logits.reshape(bsz * s, nr), n_groups, n_experts,
                       moe_w1, moe_w3, moe_w2, layer, moe_tm)
        ys = [t.reshape(bsz, s, d) for t in ys]
        gts = [t.reshape(bsz, s, 1) for t in gts]
        if last:
            (xs,) = _postnorm(xs, ys, seg_tab(layer, 5), ln_ffn_g[layer], ln_ffn_b[layer], r, n_lat, alpha,
                              rows=n_lat, row_gates=gts)
        else:
            xs, hmod = _postnorm(xs, ys, seg_tab(layer, 5), ln_ffn_g[layer], ln_ffn_b[layer], r, n_lat, alpha,
                                 shift=seg_tab(layer + 1, 0), scale=seg_tab(layer + 1, 1), row_gates=gts)
    return xs
```

```python
import functools
import math

import jax
import jax.numpy as jnp
from jax import lax
from jax.experimental import pallas as pl
from jax.experimental.pallas import tpu as pltpu

F32 = jnp.float32
BF16 = jnp.bfloat16
HIGHEST = lax.Precision.HIGHEST

LANES = 128
DN_DK = 128
DN_DV = 128
DN_CHUNK = 64
MLA_NOPE = 128
MLA_ROPE = 64
MLA_V = 128
MLA_QD = MLA_NOPE + MLA_ROPE
GRID_W = 64
ROPE_FREQS = MLA_ROPE // 4
ROPE_BASE = 10000.0
LRU_C = 8.0
LN_EPS = 1e-5
NORM_EPS = 1e-6
TOP_K = 2
FLASH_SUB = 256
VMEM_LIMIT = 48 * 1024 * 1024


def _pick(n, cands):
    for c in cands:
        if n % c == 0:
            return c
    raise ValueError(f"no tile for {n} in {cands}")


def _params(*sem):
    return pltpu.CompilerParams(dimension_semantics=sem, vmem_limit_bytes=VMEM_LIMIT)


def _dot(a, b, precision=None):
    return jnp.dot(a, b, preferred_element_type=F32, precision=precision)


def _dot_nt(a, b, precision=None):
    return lax.dot_general(a, b, (((1,), (1,)), ((), ())), preferred_element_type=F32, precision=precision)


def _sigmoid(x):
    return 1.0 / (1.0 + jnp.exp(-x))


def _silu(x):
    return x * _sigmoid(x)


def _softplus(x):
    return jnp.maximum(x, 0.0) + jnp.log1p(jnp.exp(-jnp.abs(x)))


def _expm1_nonpos(x):
    u = jnp.exp(x)
    safe = (x > -0.5) & (u < 1.0)
    stable = (u - 1.0) * x / jnp.log(jnp.where(safe, u, 0.5))
    return jnp.where(safe, stable, jnp.where(u < 1.0, u - 1.0, x))


def _adaln_kernel(c_ref, w_ref, b_ref, o_ref):
    a = _silu(c_ref[...])
    o_ref[0] = _dot(a.astype(BF16), w_ref[0].astype(BF16)) + b_ref[0]


def _adaln(c_all, ada_w, ada_b):
    n_layer, d, n6 = ada_w.shape
    tn = _pick(n6, (1024, 512, 256, 128))
    return pl.pallas_call(
        _adaln_kernel, grid=(n_layer, n6 // tn),
        in_specs=[pl.BlockSpec((8, d), lambda l, j: (0, 0)),
                  pl.BlockSpec((1, d, tn), lambda l, j: (l, 0, j)),
                  pl.BlockSpec((1, 1, tn), lambda l, j: (l, 0, j))],
        out_specs=pl.BlockSpec((1, 8, tn), lambda l, j: (l, 0, j)),
        out_shape=jax.ShapeDtypeStruct((n_layer, 8, n6), F32),
        compiler_params=_params("arbitrary", "arbitrary"), name="adaln",
    )(c_all, ada_w, ada_b.reshape(n_layer, 1, n6))


def _seg_spec(d, n_lat_blocks):
    return pl.BlockSpec((1, 1, 1, d), lambda b, i: (b, jnp.where(i >= n_lat_blocks, 1, 0), 0, 0))


def _modulate_kernel(x_ref, sh_ref, sc_ref, h_ref):
    h_ref[0] = (x_ref[0] * (1.0 + sc_ref[0, 0]) + sh_ref[0, 0]).astype(BF16)


def _modulate(x, shift, scale, r, n_lat):
    b, s, d = x.shape
    row = pl.BlockSpec((1, r, d), lambda b_, i: (b_, i, 0))
    return pl.pallas_call(
        _modulate_kernel, grid=(b, s // r),
        in_specs=[row, _seg_spec(d, n_lat // r), _seg_spec(d, n_lat // r)],
        out_specs=row, out_shape=jax.ShapeDtypeStruct((b, s, d), BF16),
        compiler_params=_params("arbitrary", "arbitrary"), name="modulate",
    )(x, shift, scale)


def _postnorm_kernel(*refs, alpha, n_y, row_gated, with_h, with_router):
    x_ref = refs[0]
    pos = 1
    y = None
    for _ in range(n_y):
        term = refs[pos][0].astype(F32)
        pos += 1
        if row_gated:
            term = term * refs[pos][0]
            pos += 1
        y = term if y is None else y + term
    gate_ref, g_ref, b_ref = refs[pos:pos + 3]
    pos += 3
    if with_h:
        sh_ref, sc_ref = refs[pos:pos + 2]
        pos += 2
    if with_router:
        wr_ref, br_ref = refs[pos:pos + 2]
        pos += 2
    xo_ref = refs[pos]
    v = alpha * x_ref[0] + gate_ref[0, 0] * y
    mu = jnp.mean(v, axis=-1, keepdims=True)
    vc = v - mu
    var = jnp.mean(vc * vc, axis=-1, keepdims=True)
    xn = vc * lax.rsqrt(var + LN_EPS) * g_ref[...] + b_ref[...]
    xo_ref[0] = xn
    if with_h:
        h = xn * (1.0 + sc_ref[0, 0]) + sh_ref[0, 0]
        refs[pos + 1][0] = h.astype(BF16)
        if with_router:
            refs[pos + 2][0] = _dot(h, wr_ref[...], HIGHEST) + br_ref[...]


def _postnorm(x, ys, gate, ln_g, ln_b, r, n_lat, alpha, shift=None, scale=None, router=None, rows=None,
              row_gates=None):
    b, s, d = x.shape
    rows = s if rows is None else rows
    nlb = n_lat // r
    row = pl.BlockSpec((1, r, d), lambda b_, i: (b_, i, 0))
    col1 = pl.BlockSpec((1, r, 1), lambda b_, i: (b_, i, 0))
    vec = pl.BlockSpec((1, d), lambda b_, i: (0, 0))
    args, in_specs = [x], [row]
    for j, y in enumerate(ys):
        args.append(y)
        in_specs.append(row)
        if row_gates is not None:
            args.append(row_gates[j])
            in_specs.append(col1)
    args += [gate, ln_g.reshape(1, d), ln_b.reshape(1, d)]
    in_specs += [_seg_spec(d, nlb), vec, vec]
    out_shape = [jax.ShapeDtypeStruct((b, rows, d), F32)]
    out_specs = [row]
    with_h = shift is not None
    if with_h:
        args += [shift, scale]
        in_specs += [_seg_spec(d, nlb), _seg_spec(d, nlb)]
        out_shape.append(jax.ShapeDtypeStruct((b, rows, d), BF16))
        out_specs.append(row)
    if router is not None:
        w_r, b_r = router
        nr = w_r.shape[1]
        args += [w_r, b_r]
        in_specs += [pl.BlockSpec((d, nr), lambda b_, i: (0, 0)), pl.BlockSpec((1, nr), lambda b_, i: (0, 0))]
        out_shape.append(jax.ShapeDtypeStruct((b, rows, nr), F32))
        out_specs.append(pl.BlockSpec((1, r, nr), lambda b_, i: (b_, i, 0)))
    return pl.pallas_call(
        functools.partial(_postnorm_kernel, alpha=alpha, n_y=len(ys), row_gated=row_gates is not None,
                          with_h=with_h, with_router=router is not None),
        grid=(b, rows // r), in_specs=in_specs, out_specs=out_specs, out_shape=out_shape,
        compiler_params=_params("arbitrary", "arbitrary"), name="postnorm",
    )(*args)


def _mm_kernel(a_ref, b_ref, o_ref):
    o_ref[...] = _dot(a_ref[...].astype(BF16), b_ref[...].astype(BF16)).astype(o_ref.dtype)


def _mm(a, b, out_dtype=F32):
    m, k = a.shape
    n = b.shape[1]
    tm = _pick(m, (512, 256, 128, 64, 8))
    tn = _pick(n, (1024, 896, 768, 640, 512, 384, 256, 128))
    return pl.pallas_call(
        _mm_kernel, grid=(m // tm, n // tn),
        in_specs=[pl.BlockSpec((tm, k), lambda i, j: (i, 0)), pl.BlockSpec((k, tn), lambda i, j: (0, j))],
        out_specs=pl.BlockSpec((tm, tn), lambda i, j: (i, j)),
        out_shape=jax.ShapeDtypeStruct((m, n), out_dtype),
        compiler_params=_params("arbitrary", "arbitrary"), name="mm",
    )(a, b)


def _even_prep_kernel(qkv_ref, qkv_p_ref, qkv_n_ref, xr_ref, xr_p_ref, xr_n_ref, ab_ref,
                      cw_ref, xw_ref, xb_ref, nea_ref, dtb_ref,
                      q_ref, k_ref, v_ref, u_ref, gb_ref, *, n_lat_blocks, n_blocks, heads, r):
    i = pl.program_id(1)
    pv = jnp.where((i != 0) & (i != n_lat_blocks), 1.0, 0.0)
    nv = jnp.where((i != n_lat_blocks - 1) & (i != n_blocks - 1), 1.0, 0.0)
    row = lax.broadcasted_iota(jnp.int32, (r, 1), 0)

    def conv(x, p8, n8, w):
        p8 = p8 * pv
        n8 = n8 * nv
        xm1 = jnp.where(row == 0, p8[7:8], pltpu.roll(x, 1, 0))
        xm2 = jnp.where(row == 0, p8[6:7], jnp.where(row == 1, p8[7:8], pltpu.roll(x, 2, 0)))
        xp1 = jnp.where(row == r - 1, n8[0:1], pltpu.roll(x, r - 1, 0))
        return w[0:1] * xm2 + w[1:2] * xm1 + w[2:3] * x + w[3:4] * xp1

    for j in range(3 * heads):
        sl = slice(j * LANES, (j + 1) * LANES)
        y = _silu(conv(qkv_ref[0, :, sl], qkv_p_ref[0, :, sl], qkv_n_ref[0, :, sl], cw_ref[:, sl]))
        if j < 2 * heads:
            y = y * lax.rsqrt(jnp.sum(y * y, axis=-1, keepdims=True) + NORM_EPS)
        if j < heads:
            q_ref[0, :, sl] = y * (DN_DK ** -0.5)
        elif j < 2 * heads:
            k_ref[0, :, slice((j - heads) * LANES, (j - heads + 1) * LANES)] = y
        else:
            v_ref[0, :, slice((j - 2 * heads) * LANES, (j - 2 * heads + 1) * LANES)] = y
    for j in range(xr_ref.shape[2] // LANES):
        sl = slice(j * LANES, (j + 1) * LANES)
        u_ref[0, :, sl] = conv(xr_ref[0, :, sl], xr_p_ref[0, :, sl], xr_n_ref[0, :, sl], xw_ref[:, sl]) + xb_ref[:, sl]
    ab = ab_ref[0]
    lane = lax.broadcasted_iota(jnp.int32, ab.shape, 1)
    g = nea_ref[...] * _softplus(ab + dtb_ref[...])
    gb_ref[0] = jnp.where(lane < 2 * heads, g, _sigmoid(ab))


def _even_prep(p, conv_qkv, conv_x_w, conv_x_b, neg_exp_a, dt_bias, r, n_lat, heads, width):
    b, s, _ = p.shape
    vw = heads * DN_DV
    qkvw = 3 * vw
    assert width == vw and qkvw % width == 0
    nb, nlb, r8 = s // r, n_lat // r, r // 8
    n8 = s // 8

    def cur(wd, cb):
        return pl.BlockSpec((1, r, wd), lambda b_, i: (b_, i, cb))

    def prev(wd, cb):
        return pl.BlockSpec((1, 8, wd), lambda b_, i: (b_, jnp.maximum(i * r8 - 1, 0), cb))

    def nxt(wd, cb):
        return pl.BlockSpec((1, 8, wd), lambda b_, i: (b_, jnp.minimum((i + 1) * r8, n8 - 1), cb))

    def par(shape):
        return pl.BlockSpec(shape, lambda b_, i: (0, 0))

    xcb = (qkvw + vw) // width
    abcb = (qkvw + vw + 2 * width) // LANES
    pad = LANES - 2 * heads
    nea = jnp.pad(neg_exp_a.reshape(1, 2 * heads), ((0, 0), (0, pad)))
    dtb = jnp.pad(dt_bias.reshape(1, 2 * heads), ((0, 0), (0, pad)))
    outs = [jax.ShapeDtypeStruct((b, s, vw), F32)] * 3 + [jax.ShapeDtypeStruct((b, s, width), F32),
                                                            jax.ShapeDtypeStruct((b, s, LANES), F32)]
    return pl.pallas_call(
        functools.partial(_even_prep_kernel, n_lat_blocks=nlb, n_blocks=nb, heads=heads, r=r),
        grid=(b, nb),
        in_specs=[cur(qkvw, 0), prev(qkvw, 0), nxt(qkvw, 0), cur(width, xcb), prev(width, xcb), nxt(width, xcb),
                  cur(LANES, abcb), par((4, qkvw)), par((4, width)), par((1, width)), par((1, LANES)), par((1, LANES))],
        out_specs=[cur(vw, 0), cur(vw, 0), cur(vw, 0), cur(width, 0), cur(LANES, 0)],
        out_shape=outs, compiler_params=_params("arbitrary", "arbitrary"), name="even_prep",
    )(p, p, p, p, p, p, p, conv_qkv, conv_x_w, conv_x_b.reshape(1, width), nea, dtb)


def _split2(x):
    hi = x.astype(BF16)
    return hi, (x - hi.astype(F32)).astype(BF16)


DN_CHUNKS_PER_STEP = 2
DN_PACK = 4


def _block_diag(x, row_blk, lane_blk, pack):
    tiled = jnp.concatenate([x] * pack, axis=0)
    return jnp.where(row_blk == lane_blk, tiled, 0.0).astype(BF16)


def _dot_hl(a, b_bd):
    n = a[0].shape[0]
    r = _dot(jnp.concatenate([a[0], a[1]], axis=0), b_bd)
    return r[:n] + r[n:]


def _delta_local_kernel(q_ref, k_ref, v_ref, g_ref, u_ref, wq_ref, ak_ref, gt_ref, *, heads, pack, cpb):
    c = DN_CHUNK
    wa = pack * c
    wk = pack * LANES
    ri = lax.broadcasted_iota(jnp.int32, (c, wa), 0)
    la = lax.broadcasted_iota(jnp.int32, (c, wa), 1)
    ci = la & (c - 1)
    blk_a = la >> 6
    blk_k = lax.broadcasted_iota(jnp.int32, (c, wk), 1) >> 7
    rb_a = lax.broadcasted_iota(jnp.int32, (wa, wa), 0) >> 6
    lb_a = lax.broadcasted_iota(jnp.int32, (wa, wa), 1) >> 6
    rb_k = lax.broadcasted_iota(jnp.int32, (wa, wk), 0) >> 6
    lb_k = lax.broadcasted_iota(jnp.int32, (wa, wk), 1) >> 7
    eye = jnp.where(ri == ci, 1.0, 0.0)
    ri1 = lax.broadcasted_iota(jnp.int32, (c, c), 0)
    ci1 = lax.broadcasted_iota(jnp.int32, (c, c), 1)

    def bd_a(x):
        return _block_diag(x, rb_a, lb_a, pack)

    def bd_k(x):
        return _block_diag(x, rb_k, lb_k, pack)

    def per_head(cols, blk):
        out = cols[0]
        for t in range(1, pack):
            out = jnp.where(blk == t, cols[t], out)
        return out

    incl = [(ri >= ci), (ri <= ci)]
    strict = [(ri > ci), (ri < ci)]
    last = [c - 1, 0]
    levels = [[strict[d] & ((ri >> (k + 1)) == (ci >> (k + 1))) & ((ri >> k) != (ci >> k)) for k in range(6)]
              for d in range(2)]
    ones = [jnp.where(ri1 >= ci1, 1.0, 0.0).astype(BF16), jnp.where(ri1 <= ci1, 1.0, 0.0).astype(BF16)]

    packs = []
    for j in range(cpb):
        rows = slice(j * c, (j + 1) * c)
        gall = g_ref[0, rows]
        g1 = gall.astype(BF16)
        rem = gall - g1.astype(F32)
        g2 = rem.astype(BF16)
        g3 = (rem - g2.astype(F32)).astype(BF16)
        for d in range(2):
            gcum = _dot(ones[d], g1) + (_dot(ones[d], g2) + _dot(ones[d], g3))
            gt_ref[d, j] = jnp.exp(gcum[last[d]:last[d] + 1, :])
            for h0 in range(0, heads, pack):
                cols = [d * heads + h0 + t for t in range(pack)]
                gcs = [gcum[:, cc:cc + 1] for cc in cols]
                gc_a = per_head([jnp.broadcast_to(x, (c, wa)) for x in gcs], blk_a)
                gr_a = jnp.sum(jnp.where(ri == ci, gc_a, 0.0), axis=0, keepdims=True)
                decay = jnp.where(incl[d], jnp.exp(jnp.where(incl[d], gc_a - gr_a, 0.0)), 0.0)
                beta_k = per_head([jnp.broadcast_to(gall[:, 2 * heads + cc:2 * heads + cc + 1], (c, wk))
                                   for cc in cols], blk_k)
                gc_k = per_head([jnp.broadcast_to(x, (c, wk)) for x in gcs], blk_k)
                gl_k = per_head([jnp.broadcast_to(x[last[d]:last[d] + 1], (c, wk)) for x in gcs], blk_k)
                sl = slice(h0 * LANES, (h0 + pack) * LANES)
                q_k, k_k, v_k = q_ref[0, rows, sl], k_ref[0, rows, sl], v_ref[0, rows, sl]
                kb_k = k_k * beta_k
                eg_k = jnp.exp(gc_k)
                packs.append(dict(j=j, d=d, h0=h0, decay=decay, kb=kb_k, q=q_k, k=k_k, eg=eg_k,
                                  rhs_u=v_k * beta_k, rhs_w=kb_k * eg_k, kd=k_k * jnp.exp(gl_k - gc_k)))

    for p in packs:
        lhs = jnp.concatenate([p["kb"], p["q"]], axis=0).astype(BF16)
        p["aa"] = _dot_nt(lhs, bd_k(p["k"]))
    for p in packs:
        d = p["d"]
        p["a"] = jnp.where(strict[d], p["aa"][:c] * p["decay"], 0.0).astype(BF16).astype(F32)
        p["a_qk"] = jnp.where(incl[d], p["aa"][c:] * p["decay"], 0.0)
        p["t"] = eye - jnp.where(levels[d][0], p["a"], 0.0)
    for k in range(1, 6):
        for p in packs:
            p["ts"] = _split2(p["t"])
            p["x"] = _dot_hl(p["ts"], bd_a(jnp.where(levels[p["d"]][k], p["a"], 0.0)))
        for p in packs:
            xs = _split2(p["x"])
            y = _dot_hl(xs, bd_a(p["ts"][0].astype(F32))) + _dot(xs[0], bd_a(p["ts"][1].astype(F32)))
            p["t"] = p["t"] - y
    for p in packs:
        p["ts"] = _split2(p["t"])
    for name in ("u", "w"):
        for p in packs:
            rh, rl = _split2(p["rhs_" + name])
            p[name] = _dot_hl(p["ts"], bd_k(rh.astype(F32))) + _dot(p["ts"][0], bd_k(rl.astype(F32)))
    for p in packs:
        j, d = p["j"], p["d"]
        qd_k = p["q"] * p["eg"]
        for t in range(pack):
            h = p["h0"] + t
            ks = slice(t * LANES, (t + 1) * LANES)
            u_ref[d, j, h] = p["u"][:, ks]
            wq_ref[d, j, h, 0:c] = p["w"][:, ks].astype(BF16)
            wq_ref[d, j, h, c:2 * c] = qd_k[:, ks].astype(BF16)
            ak_ref[d, j, h, 0:c] = p["a_qk"][:, t * c:(t + 1) * c].astype(BF16)
            ak_ref[d, j, h, c:c + DN_DK] = p["kd"][:, ks].T.astype(BF16)


def _delta_scan_kernel(uf, wqf, akf, gtf, ub, wqb, akb, gtb, of, ob, state, *, heads, cpb):
    c = DN_CHUNK

    @pl.when(pl.program_id(1) == 0)
    def _():
        state[...] = jnp.zeros_like(state)

    dirs = ((uf, wqf, akf, gtf, of), (ub, wqb, akb, gtb, ob))
    chains = [(d, h) for d in range(2) for h in range(heads)]
    s_cur = {ch: state[ch[0], ch[1]] for ch in chains}
    for step in range(cpb):
        at = (step, cpb - 1 - step)
        r1 = {(d, h): _dot(dirs[d][1][at[d], h], s_cur[(d, h)].astype(BF16)) for d, h in chains}
        r2 = {}
        for d, h in chains:
            v_new = dirs[d][0][at[d], h] - r1[(d, h)][:c]
            r2[(d, h)] = _dot(dirs[d][2][at[d], h], v_new.astype(BF16))
        for d, h in chains:
            gt = dirs[d][3][at[d]]
            dirs[d][4][0, at[d] * c:(at[d] + 1) * c, h * LANES:(h + 1) * LANES] = r1[(d, h)][c:] + r2[(d, h)][:c]
            s_cur[(d, h)] = s_cur[(d, h)] * gt[:, d * heads + h:d * heads + h + 1] + r2[(d, h)][c:]
    for d, h in chains:
        state[d, h] = s_cur[(d, h)]


def _delta(q, k, v, gates, n_lat, heads):
    b, s, vw = q.shape
    c = DN_CHUNK
    n_chunks, n_lat_c = s // c, n_lat // c
    n_ctx_c = n_chunks - n_lat_c
    cpb = _pick(math.gcd(n_lat_c, n_ctx_c), (DN_CHUNKS_PER_STEP, 1))
    seq = pl.BlockSpec((1, cpb * c, vw), lambda b_, i: (b_, i, 0))

    def loc(rows, cols):
        return pl.BlockSpec((2, None, cpb, heads, rows, cols), lambda b_, i: (0, b_, i, 0, 0, 0))

    u, wq, ak, gt = pl.pallas_call(
        functools.partial(_delta_local_kernel, heads=heads, pack=math.gcd(heads, DN_PACK), cpb=cpb),
        grid=(b, n_chunks // cpb),
        in_specs=[seq, seq, seq, pl.BlockSpec((1, cpb * c, LANES), lambda b_, i: (b_, i, 0))],
        out_specs=[loc(c, DN_DV), loc(2 * c, DN_DV), loc(c + DN_DK, c),
                   pl.BlockSpec((2, None, cpb, 1, LANES), lambda b_, i: (0, b_, i, 0, 0))],
        out_shape=[jax.ShapeDtypeStruct((2, b, n_chunks, heads, c, DN_DV), F32),
                   jax.ShapeDtypeStruct((2, b, n_chunks, heads, 2 * c, DN_DV), BF16),
                   jax.ShapeDtypeStruct((2, b, n_chunks, heads, c + DN_DK, c), BF16),
                   jax.ShapeDtypeStruct((2, b, n_chunks, 1, LANES), F32)],
        compiler_params=_params("arbitrary", "arbitrary"), name="delta_local",
    )(q, k, v, gates)

    n_blk, n_lat_b = n_chunks // cpb, n_lat_c // cpb
    n_ctx_b = n_blk - n_lat_b

    def fwd(st):
        return jnp.where(st < n_ctx_b, n_lat_b + st, st - n_ctx_b)

    def bwd(st):
        return jnp.where(st < n_ctx_b, n_blk - 1 - st, n_lat_b - 1 - (st - n_ctx_b))

    def chunk(d, order, rows, cols):
        return pl.BlockSpec((None, None, cpb, heads, rows, cols), lambda b_, st: (d, b_, order(st), 0, 0, 0))

    def gspec(d, order):
        return pl.BlockSpec((None, None, cpb, 1, LANES), lambda b_, st: (d, b_, order(st), 0, 0))

    def ospec(order):
        return pl.BlockSpec((1, cpb * c, vw), lambda b_, st: (b_, order(st), 0))

    ins, args = [], []
    for d, order in ((0, fwd), (1, bwd)):
        ins += [chunk(d, order, c, DN_DV), chunk(d, order, 2 * c, DN_DV), chunk(d, order, c + DN_DK, c),
                gspec(d, order)]
        args += [u, wq, ak, gt]
    out = jax.ShapeDtypeStruct((b, s, vw), F32)
    return pl.pallas_call(
        functools.partial(_delta_scan_kernel, heads=heads, cpb=cpb), grid=(b, n_blk),
        in_specs=ins, out_specs=[ospec(fwd), ospec(bwd)], out_shape=[out, out],
        scratch_shapes=[pltpu.VMEM((2, heads, DN_DK, DN_DV), F32)],
        compiler_params=_params("arbitrary", "arbitrary"), name="delta_scan",
    )(*args)


def _lru_kernel(uf_ref, ub_ref, wri_ref, bri_ref, spl_ref, hf_ref, hb_ref, carry, *, r, n_blk):
    @pl.when(pl.program_id(1) == 0)
    def _():
        carry[...] = jnp.zeros_like(carry)

    row = lax.broadcasted_iota(jnp.int32, (r, 1), 0)
    for d, (u_ref, h_ref) in enumerate(((uf_ref, hf_ref), (ub_ref, hb_ref))):
        for n in range(n_blk):
            sl = slice(n * LANES, (n + 1) * LANES)
            u = u_ref[0, :, sl]
            ri = _dot(u.astype(BF16), wri_ref[d, n]) + bri_ref[d, n]
            rg = _sigmoid(ri[:, :LANES])
            ig = _sigmoid(ri[:, LANES:])
            log_a = -spl_ref[d, n] * rg
            a = jnp.exp(log_a)
            bt = jnp.sqrt(-_expm1_nonpos(2.0 * log_a)) * (ig * u)
            sh = 1
            while sh < r:
                if d == 0:
                    keep = row >= sh
                    a_s, b_s = pltpu.roll(a, sh, 0), pltpu.roll(bt, sh, 0)
                else:
                    keep = row < r - sh
                    a_s, b_s = pltpu.roll(a, r - sh, 0), pltpu.roll(bt, r - sh, 0)
                bt = a * jnp.where(keep, b_s, 0.0) + bt
                a = a * jnp.where(keep, a_s, 1.0)
                sh *= 2
            h = bt + a * carry[d, :, sl]
            h_ref[0, :, sl] = h
            carry[d, :, sl] = h[r - 1:r] if d == 0 else h[0:1]


def _lru(u, w_ri, b_ri, spl, r, n_lat):
    b, s, width = u.shape
    n_blk = width // LANES
    nb, nlb = s // r, n_lat // r
    ncb = nb - nlb

    def fwd(st):
        return jnp.where(st < ncb, nlb + st, st - ncb)

    def bwd(st):
        return jnp.where(st < ncb, nb - 1 - st, nlb - 1 - (st - ncb))

    def seq(order):
        return pl.BlockSpec((1, r, width), lambda b_, st: (b_, order(st), 0))

    def par(shape):
        return pl.BlockSpec(shape, lambda b_, st: (0,) * len(shape))

    out = jax.ShapeDtypeStruct((b, s, width), F32)
    return pl.pallas_call(
        functools.partial(_lru_kernel, r=r, n_blk=n_blk), grid=(b, nb),
        in_specs=[seq(fwd), seq(bwd), par(w_ri.shape), par(b_ri.shape), par(spl.shape)],
        out_specs=[seq(fwd), seq(bwd)], out_shape=[out, out],
        scratch_shapes=[pltpu.VMEM((2, 1, width), F32)],
        compiler_params=_params("arbitrary", "arbitrary"), name="lru",
    )(u, u, w_ri, b_ri, spl)


def _gelu_tanh(x):
    return 0.5 * x * (1.0 + jnp.tanh(math.sqrt(2.0 / math.pi) * (x + 0.044715 * (x * x * x))))


def _even_out_kernel(of_ref, ob_ref, z_ref, hf_ref, hb_ref, y_ref, on_ref, mix_ref, *, heads, n_blk):
    for h in range(heads):
        sl = slice(h * LANES, (h + 1) * LANES)
        o = of_ref[0, :, sl] + ob_ref[0, :, sl]
        o = o * lax.rsqrt(jnp.mean(o * o, axis=-1, keepdims=True) + NORM_EPS) * on_ref[...]
        mix_ref[0, :, sl] = (o * _silu(z_ref[0, :, sl])).astype(BF16)
    for n in range(n_blk):
        sl = slice(n * LANES, (n + 1) * LANES)
        osl = slice((heads + n) * LANES, (heads + n + 1) * LANES)
        mix_ref[0, :, osl] = ((hf_ref[0, :, sl] + hb_ref[0, :, sl]) * _gelu_tanh(y_ref[0, :, sl])).astype(BF16)


def _even_out(o_f, o_b, p, h_f, h_b, o_norm, r, heads):
    b, s, vw = o_f.shape
    width = h_f.shape[2]

    def cur(wd, cb):
        return pl.BlockSpec((1, r, wd), lambda b_, i: (b_, i, cb))

    zcb = (3 * vw) // vw
    ycb = (4 * vw + width) // width
    return pl.pallas_call(
        functools.partial(_even_out_kernel, heads=heads, n_blk=width // LANES), grid=(b, s // r),
        in_specs=[cur(vw, 0), cur(vw, 0), cur(vw, zcb), cur(width, 0), cur(width, 0), cur(width, ycb),
                  pl.BlockSpec((1, DN_DV), lambda b_, i: (0, 0))],
        out_specs=cur(vw + width, 0), out_shape=jax.ShapeDtypeStruct((b, s, vw + width), BF16),
        compiler_params=_params("arbitrary", "arbitrary"), name="even_out",
    )(o_f, o_b, p, h_f, h_b, p, o_norm.reshape(1, DN_DV))


def _rope_pair(y, tab):
    y = y * tab
    return y + pltpu.roll(y, MLA_ROPE, 1)


def _q_proj_kernel(a_ref, w_ref, tab_ref, q_ref, *, scale):
    acc = _dot(a_ref[0], w_ref[0])
    q_ref[0, 0, :, 0:MLA_NOPE] = (acc[:, :MLA_NOPE] * scale).astype(BF16)
    qr = _rope_pair(acc[:, MLA_NOPE:], tab_ref[...])
    q_ref[0, 0, :, MLA_NOPE:MLA_QD] = (qr[:, :MLA_ROPE] * scale).astype(BF16)


def _q_proj(h, w_q, tab, heads):
    b, s, d = h.shape
    tm = _pick(s, (768, 512, 640, 256, 128))
    return pl.pallas_call(
        functools.partial(_q_proj_kernel, scale=math.log2(math.e) * MLA_QD ** -0.5), grid=(b, s // tm, heads),
        in_specs=[pl.BlockSpec((1, tm, d), lambda b_, i, hd: (b_, i, 0)),
                  pl.BlockSpec((1, d, 2 * LANES), lambda b_, i, hd: (hd, 0, 0)),
                  pl.BlockSpec((tm, LANES), lambda b_, i, hd: (i, 0))],
        out_specs=pl.BlockSpec((1, 1, tm, MLA_QD), lambda b_, i, hd: (b_, hd, i, 0)),
        out_shape=jax.ShapeDtypeStruct((b, heads, s, MLA_QD), BF16),
        compiler_params=_params("arbitrary", "arbitrary", "arbitrary"), name="q_proj",
    )(h, w_q, tab)


def _ckv_proj_kernel(a_ref, w_ref, tab_ref, g_ref, ckv_ref, kr_ref, *, rank):
    acc = _dot(a_ref[0], w_ref[...])
    ckv = acc[:, :rank]
    ckv = ckv * lax.rsqrt(jnp.mean(ckv * ckv, axis=-1, keepdims=True) + NORM_EPS) * g_ref[...]
    ckv_ref[0] = ckv.astype(BF16)
    kr_ref[0] = _rope_pair(acc[:, rank:], tab_ref[...]).astype(BF16)


def _ckv_proj(h, w_c, tab, kv_norm):
    b, s, d = h.shape
    rank = kv_norm.shape[0]
    tm = _pick(s, (768, 512, 640, 256, 128))
    return pl.pallas_call(
        functools.partial(_ckv_proj_kernel, rank=rank), grid=(b, s // tm),
        in_specs=[pl.BlockSpec((1, tm, d), lambda b_, i: (b_, i, 0)),
                  pl.BlockSpec((d, rank + LANES), lambda b_, i: (0, 0)),
                  pl.BlockSpec((tm, LANES), lambda b_, i: (i, 0)),
                  pl.BlockSpec((1, rank), lambda b_, i: (0, 0))],
        out_specs=[pl.BlockSpec((1, tm, rank), lambda b_, i: (b_, i, 0)),
                   pl.BlockSpec((1, tm, LANES), lambda b_, i: (b_, i, 0))],
        out_shape=[jax.ShapeDtypeStruct((b, s, rank), BF16), jax.ShapeDtypeStruct((b, s, LANES), BF16)],
        compiler_params=_params("arbitrary", "arbitrary"), name="ckv_proj",
    )(h, w_c, tab, kv_norm.reshape(1, rank))


def _kv_up_kernel(a_ref, w_ref, kr_ref, k_ref, v_ref):
    acc = _dot(a_ref[0], w_ref[0])
    k_ref[0, 0, :, 0:MLA_NOPE] = acc[:, :MLA_NOPE].astype(BF16)
    k_ref[0, 0, :, MLA_NOPE:MLA_QD] = kr_ref[0, :, 0:MLA_ROPE]
    v_ref[0, 0] = acc[:, MLA_NOPE:].astype(BF16)


def _kv_up(ckv, w_ukv, kr, heads):
    b, s, rank = ckv.shape
    tm = _pick(s, (768, 512, 640, 256, 128))
    return pl.pallas_call(
        _kv_up_kernel, grid=(b, s // tm, heads),
        in_specs=[pl.BlockSpec((1, tm, rank), lambda b_, i, hd: (b_, i, 0)),
                  pl.BlockSpec((1, rank, MLA_NOPE + MLA_V), lambda b_, i, hd: (hd, 0, 0)),
                  pl.BlockSpec((1, tm, LANES), lambda b_, i, hd: (b_, i, 0))],
        out_specs=[pl.BlockSpec((1, 1, tm, MLA_QD), lambda b_, i, hd: (b_, hd, i, 0)),
                   pl.BlockSpec((1, 1, tm, MLA_V), lambda b_, i, hd: (b_, hd, i, 0))],
        out_shape=[jax.ShapeDtypeStruct((b, heads, s, MLA_QD), BF16),
                   jax.ShapeDtypeStruct((b, heads, s, MLA_V), BF16)],
        compiler_params=_params("arbitrary", "arbitrary", "arbitrary"), name="kv_up",
    )(ckv, w_ukv, kr)


def _flash_kernel(q_ref, k_ref, v_ref, o_ref, m_ref, l_ref, acc_ref, *, sub):
    j = pl.program_id(3)

    @pl.when(j == 0)
    def _():
        m_ref[...] = jnp.full_like(m_ref, -jnp.inf)
        l_ref[...] = jnp.zeros_like(l_ref)
        acc_ref[...] = jnp.zeros_like(acc_ref)

    k = k_ref[0, 0]
    v = v_ref[0, 0]
    for c in range(q_ref.shape[2] // sub):
        rows = slice(c * sub, (c + 1) * sub)
        s = _dot_nt(q_ref[0, 0, rows, :], k)
        m_prev = m_ref[rows]
        m_new = jnp.maximum(m_prev, jnp.max(s, axis=-1, keepdims=True))
        p = jnp.exp2(s - m_new)
        alpha = jnp.exp2(m_prev - m_new)
        l_ref[rows] = alpha * l_ref[rows] + jnp.sum(p, axis=-1, keepdims=True)
        acc_ref[rows] = alpha * acc_ref[rows] + _dot(p.astype(BF16), v)
        m_ref[rows] = m_new

    @pl.when(j == pl.num_programs(3) - 1)
    def _():
        o_ref[0] = (acc_ref[...] / l_ref[...]).astype(o_ref.dtype)


def _attn_full_kernel(q_ref, k_ref, v_ref, o_ref, *, sub):
    k = k_ref[0, 0]
    v = v_ref[0, 0]
    for c in range(q_ref.shape[2] // sub):
        rows = slice(c * sub, (c + 1) * sub)
        s = _dot_nt(q_ref[0, 0, rows, :], k)
        p = jnp.exp2(s - jnp.max(s, axis=-1, keepdims=True))
        l = jnp.sum(p, axis=-1, keepdims=True)
        o_ref[0, rows, :] = (_dot(p.astype(BF16), v) / l).astype(o_ref.dtype)


def _flash(q, k, v, q_rows, kv_rows):
    b, heads, _, dq = q.shape
    dv = v.shape[3]
    q0, nq = q_rows
    k0, nk = kv_rows
    tq = _pick(math.gcd(nq, q0) if q0 else nq, (2048, 1024, 512, 256, 128))
    tk = _pick(math.gcd(nk, k0) if k0 else nk, (nk, 768, 512, 256, 128))
    qo, ko = q0 // tq, k0 // tk
    if nk == tk:
        return pl.pallas_call(
            functools.partial(_attn_full_kernel, sub=min(tq, FLASH_SUB)), grid=(b, heads, nq // tq),
            in_specs=[pl.BlockSpec((1, 1, tq, dq), lambda b_, h, i: (b_, h, qo + i, 0)),
                      pl.BlockSpec((1, 1, tk, dq), lambda b_, h, i: (b_, h, ko, 0)),
                      pl.BlockSpec((1, 1, tk, dv), lambda b_, h, i: (b_, h, ko, 0))],
            out_specs=pl.BlockSpec((1, tq, dv), lambda b_, h, i: (b_, i, h)),
            out_shape=jax.ShapeDtypeStruct((b, nq, heads * dv), BF16),
            compiler_params=_params("arbitrary", "arbitrary", "arbitrary"), name="attn_full",
        )(q, k, v)
    return pl.pallas_call(
        functools.partial(_flash_kernel, sub=min(tq, FLASH_SUB)), grid=(b, heads, nq // tq, nk // tk),
        in_specs=[pl.BlockSpec((1, 1, tq, dq), lambda b_, h, i, j: (b_, h, qo + i, 0)),
                  pl.BlockSpec((1, 1, tk, dq), lambda b_, h, i, j: (b_, h, ko + j, 0)),
                  pl.BlockSpec((1, 1, tk, dv), lambda b_, h, i, j: (b_, h, ko + j, 0))],
        out_specs=pl.BlockSpec((1, tq, dv), lambda b_, h, i, j: (b_, i, h)),
        out_shape=jax.ShapeDtypeStruct((b, nq, heads * dv), BF16),
        scratch_shapes=[pltpu.VMEM((tq, 1), F32), pltpu.VMEM((tq, 1), F32), pltpu.VMEM((tq, dv), F32)],
        compiler_params=_params("arbitrary", "arbitrary", "arbitrary", "arbitrary"), name="flash",
    )(q, k, v)


def _experts_kernel(be_ref, nu_ref, x_ref, w1_ref, w3_ref, w2_ref, y_ref):
    @pl.when(pl.program_id(0) < nu_ref[0])
    def _():
        x = x_ref[...].astype(BF16)
        h1 = _dot(x, w1_ref[0].astype(BF16))
        h3 = _dot(x, w3_ref[0].astype(BF16))
        hh = (_silu(h1) * h3).astype(BF16)
        y_ref[...] = _dot(hh, w2_ref[0].astype(BF16)).astype(y_ref.dtype)


def _experts(xg, blk_e, n_used, w1, w3, w2, layer, tm):
    rows, d = xg.shape
    de = w1.shape[3]
    def used(i, nu):
        return jnp.minimum(i, nu[0] - 1)

    grid_spec = pltpu.PrefetchScalarGridSpec(
        num_scalar_prefetch=2, grid=(rows // tm,),
        in_specs=[pl.BlockSpec((tm, d), lambda i, be, nu: (used(i, nu), 0)),
                  pl.BlockSpec((None, 1, d, de), lambda i, be, nu: (layer, be[used(i, nu)], 0, 0)),
                  pl.BlockSpec((None, 1, d, de), lambda i, be, nu: (layer, be[used(i, nu)], 0, 0)),
                  pl.BlockSpec((None, 1, de, d), lambda i, be, nu: (layer, be[used(i, nu)], 0, 0))],
        out_specs=pl.BlockSpec((tm, d), lambda i, be, nu: (used(i, nu), 0)))
    return pl.pallas_call(
        _experts_kernel, grid_spec=grid_spec, out_shape=jax.ShapeDtypeStruct((rows, d), BF16),
        compiler_params=_params("arbitrary"), name="experts",
    )(blk_e, n_used, xg, w1, w3, w2)


def _moe(h, logits, n_groups, n_experts, w1, w3, w2, layer, tm):
    n_tok, d = h.shape
    epg = n_experts // n_groups
    lg = logits[:, :n_groups]
    grp = jnp.argmax(lg, axis=-1)
    p_grp = jnp.take_along_axis(jax.nn.softmax(lg, -1), grp[:, None], -1)
    le = logits[:, n_groups:n_groups + n_experts].reshape(n_tok, n_groups, epg)
    le = jnp.take_along_axis(le, grp[:, None, None], axis=1)[:, 0]
    top_v, top_i = lax.top_k(le, TOP_K)
    gate = p_grp * jax.nn.softmax(top_v, -1)
    expert = (grp[:, None] * epg + top_i).reshape(-1).astype(jnp.int32)
    onehot = (expert[:, None] == jnp.arange(n_experts, dtype=jnp.int32)[None, :]).astype(jnp.int32)
    rank = jnp.sum((jnp.cumsum(onehot, axis=0) - onehot) * onehot, axis=1)
    counts = jnp.sum(onehot, axis=0)
    padded = (counts + tm - 1) // tm * tm
    pend = jnp.cumsum(padded)
    dest = (pend - padded)[expert] + rank
    n_blk = (n_tok * TOP_K + tm - 1) // tm + n_experts
    blk_e = jnp.minimum(jnp.sum(pend[None, :] <= (jnp.arange(n_blk) * tm)[:, None], axis=1), n_experts - 1)
    n_used = (pend[-1] // tm).reshape(1)
    src = (jnp.arange(n_blk * tm, dtype=jnp.int32) % n_tok).at[dest].set(
        jnp.arange(n_tok * TOP_K, dtype=jnp.int32) // TOP_K)
    y = _experts(h[src], blk_e.astype(jnp.int32), n_used.astype(jnp.int32), w1, w3, w2, layer, tm)
    dest = dest.reshape(n_tok, TOP_K)
    return (y[dest[:, 0]], y[dest[:, 1]]), (gate[:, 0:1], gate[:, 1:2])


def _rope_table(n_lat, n_ctx):
    rows = n_lat // GRID_W
    row = jnp.broadcast_to(jnp.arange(rows)[:, None], (rows, GRID_W)).reshape(-1)
    col = jnp.broadcast_to(jnp.arange(GRID_W)[None, :], (rows, GRID_W)).reshape(-1)
    pos = jnp.stack([row, col], -1).astype(F32)
    inv = ROPE_BASE ** (-jnp.arange(ROPE_FREQS, dtype=F32) / ROPE_FREQS)
    ang = pos[:, :, None] * inv
    ang = jnp.broadcast_to(ang[:, :, None, :], (n_lat, 2, 2, ROPE_FREQS)).reshape(n_lat, MLA_ROPE)
    lat = jnp.concatenate([jnp.cos(ang), jnp.sin(ang)], axis=-1)
    ctx = jnp.concatenate([jnp.ones((n_ctx, MLA_ROPE), F32), jnp.zeros((n_ctx, MLA_ROPE), F32)], axis=-1)
    return jnp.concatenate([lat, ctx], axis=0)


def _rot_cols(w):
    wr = w.reshape(w.shape[:-1] + (2, 2, ROPE_FREQS))
    return jnp.stack([-wr[..., 1, :], wr[..., 0, :]], axis=-2).reshape(w.shape)


def kernel(x, c, ctx, c_ctx, ada_w, ada_b, ln_mix_g, ln_mix_b, ln_ffn_g, ln_ffn_b, ev_w_in, ev_conv_qkv, ev_a_log,
           ev_dt_bias, ev_o_norm, ev_conv_x_w, ev_conv_x_b, ev_w_r, ev_b_r, ev_w_i, ev_b_i, ev_lam, ev_w_out,
           od_w_in, od_kv_norm, od_w_ukv, od_w_out, moe_w_grp, moe_b_grp, moe_w_exp, moe_b_exp, moe_w1, moe_w3,
           moe_w2):
    bsz, n_lat, d = x.shape
    n_ctx = ctx.shape[1]
    s = n_lat + n_ctx
    depth = ada_w.shape[0]
    alpha = (2.0 * depth) ** 0.25
    r = min(256, n_ctx)
    assert n_lat % r == 0 and n_ctx % r == 0 and n_lat % GRID_W == 0 and bsz + 1 <= 8
    dn_heads = ev_a_log.shape[-1]
    vw = dn_heads * DN_DV
    width = ev_lam.shape[-1]
    lru_blocks = ev_w_r.shape[2]
    assert width // lru_blocks == LANES
    rank = od_kv_norm.shape[-1]
    mla_heads = od_w_ukv.shape[-1] // (MLA_NOPE + MLA_V)
    n_groups, n_experts = moe_w_grp.shape[-1], moe_w_exp.shape[-1]
    rows_per_expert = (bsz * s * TOP_K) // n_experts
    moe_tm = 512 if rows_per_expert >= 1024 else (256 if rows_per_expert >= 512 else 128)

    xs = jnp.concatenate([x, ctx], axis=1)
    c_all = jnp.zeros((8, d), F32).at[:bsz].set(c).at[bsz].set(c_ctx)
    mods = _adaln(c_all, ada_w, ada_b).reshape(depth, 8, 6, d)

    def seg_tab(layer, k):
        lat = mods[layer, :bsz, k]
        ctx_v = jnp.broadcast_to(mods[layer, bsz, k][None], (bsz, d))
        return jnp.stack([lat, ctx_v], axis=1)[:, :, None, :]

    rope_tab = _rope_table(n_lat, n_ctx)
    hmod = _modulate(xs, seg_tab(0, 0), seg_tab(0, 1), r, n_lat)
    for layer in range(depth):
        i = layer // 2
        last = layer == depth - 1
        if layer % 2 == 0:
            qkvw = 3 * vw
            w = ev_w_in[i]
            o_z, o_a, o_b, o_x, o_y = qkvw, qkvw + vw, qkvw + vw + 2 * dn_heads, qkvw + vw + 4 * dn_heads, \
                qkvw + vw + 4 * dn_heads + width
            w_perm = jnp.concatenate([w[:, :o_a], w[:, o_x:], w[:, o_a:o_x],
                                      jnp.zeros((d, LANES - 4 * dn_heads), F32)], axis=1).astype(BF16)
            p = _mm(hmod.reshape(bsz * s, d), w_perm).reshape(bsz, s, -1)
            q, k, v, u, gb = _even_prep(p, ev_conv_qkv[i], ev_conv_x_w[i], ev_conv_x_b[i], -jnp.exp(ev_a_log[i]),
                                        ev_dt_bias[i], r, n_lat, dn_heads, width)
            o_f, o_b = _delta(q, k, v, gb, n_lat, dn_heads)
            w_ri = jnp.concatenate([ev_w_r[i], ev_w_i[i]], axis=-1).astype(BF16)
            b_ri = jnp.concatenate([ev_b_r[i].reshape(2, lru_blocks, 1, LANES),
                                    ev_b_i[i].reshape(2, lru_blocks, 1, LANES)], axis=-1)
            spl = (LRU_C * jax.nn.softplus(-ev_lam[i])).reshape(2, lru_blocks, 1, LANES)
            h_f, h_b = _lru(u, w_ri, b_ri, spl, r, n_lat)
            mix = _even_out(o_f, o_b, p, h_f, h_b, ev_o_norm[i], r, dn_heads)
            w_out = ev_w_out[i]
        else:
            w = od_w_in[i]
            nq = mla_heads * MLA_QD
            wq = w[:, :nq].reshape(d, mla_heads, MLA_QD)
            wq = jnp.concatenate([wq, _rot_cols(wq[..., MLA_NOPE:])], axis=-1).transpose(1, 0, 2).astype(BF16)
            w_kr = w[:, nq + rank:]
            w_c = jnp.concatenate([w[:, nq:nq + rank], w_kr, _rot_cols(w_kr)], axis=-1).astype(BF16)
            w_u = od_w_ukv[i].reshape(rank, mla_heads, MLA_NOPE + MLA_V).transpose(1, 0, 2).astype(BF16)
            qh = _q_proj(hmod, wq, rope_tab, mla_heads)
            ckv, kr = _ckv_proj(hmod, w_c, rope_tab, od_kv_norm[i])
            kh, vh = _kv_up(ckv, w_u, kr, mla_heads)
            mix = jnp.concatenate([_flash(qh, kh, vh, (0, n_lat), (0, s)),
                                   _flash(qh, kh, vh, (n_lat, n_ctx), (n_lat, n_ctx))], axis=1)
            w_out = od_w_out[i]
        y = _mm(mix.reshape(bsz * s, -1), w_out.astype(BF16), BF16).reshape(bsz, s, d)
        nr = LANES * ((n_groups + n_experts + LANES - 1) // LANES)
        w_rt = jnp.pad(jnp.concatenate([moe_w_grp[layer], moe_w_exp[layer]], axis=1),
                       ((0, 0), (0, nr - n_groups - n_experts)))
        b_rt = jnp.pad(jnp.concatenate([moe_b_grp[layer], moe_b_exp[layer]]), (0, nr - n_groups - n_experts))
        xs, hf, logits = _postnorm(xs, [y], seg_tab(layer, 2), ln_mix_g[layer], ln_mix_b[layer], r, n_lat, alpha,
                                   shift=seg_tab(layer, 3), scale=seg_tab(layer, 4), router=(w_rt, b_rt[None]))
        ys, gts = _moe(hf.reshape(bsz * s, d), logits.reshape(bsz * s, nr), n_groups, n_experts,
                       moe_w1, moe_w3, moe_w2, layer, moe_tm)
        ys = [t.reshape(bsz, s, d) for t in ys]
        gts = [t.reshape(bsz, s, 1) for t in gts]
        if last:
            (xs,) = _postnorm(xs, ys, seg_tab(layer, 5), ln_ffn_g[layer], ln_ffn_b[layer], r, n_lat, alpha,
                              rows=n_lat, row_gates=gts)
        else:
            xs, hmod = _postnorm(xs, ys, seg_tab(layer, 5), ln_ffn_g[layer], ln_ffn_b[layer], r, n_lat, alpha,
                                 shift=seg_tab(layer + 1, 0), scale=seg_tab(layer + 1, 1), row_gates=gts)
    return xs
```
